```python
import jax, jax.numpy as jnp
from jax import lax
import numpy as np

D_MODEL = 1024
BATCH = 8
SEQ = 2048
DEPTH = 4
DEC_BATCH = 128
DEC_SEQ = 4
PAST_LEN = 16384
PAGE_SIZE = 128

N_MIXERS = 2
N_LRU_LAYERS = (DEPTH + 1) // 2
N_SG_LAYERS = DEPTH // 2
D_RNN = 3 * D_MODEL // 2
N_LRU_HEADS = 16
LRU_BLOCK = D_RNN // N_LRU_HEADS
CONV_W = 4
LRU_C = 8.0
CHUNK = 128
D_SG = 3 * D_MODEL // 2
N_SG_GROUPS = 16
SG_GROUP = D_SG // N_SG_GROUPS
D_FF = 4 * D_MODEL
EPS = 1e-6

kernel_name = "hawk_gmlp_interleaved_decoder_step"


def rmsnorm(x, g):
    xf = x.astype(jnp.float32)
    y = xf * lax.rsqrt(jnp.mean(xf * xf, axis=-1, keepdims=True) + EPS)
    return (y * g.astype(jnp.float32)).astype(x.dtype)


def layernorm(x, g, b):
    xf = x.astype(jnp.float32)
    mu = jnp.mean(xf, axis=-1, keepdims=True)
    xc = xf - mu
    var = jnp.mean(xc * xc, axis=-1, keepdims=True)
    y = xc * lax.rsqrt(var + EPS) * g.astype(jnp.float32) + b.astype(jnp.float32)
    return y.astype(x.dtype)


def ada_params(c, w, b):
    mod = jax.nn.silu(c) @ w + b
    return jnp.split(mod[:, None, :], 6, axis=-1)


def causal_conv(xb, buf, w, b):
    t = xb.shape[1]
    xp = jnp.concatenate([buf.astype(xb.dtype), xb], axis=1)
    y = b
    for k in range(CONV_W):
        y = y + xp[:, k:k + t] * w[k]
    return y, xp[:, -(CONV_W - 1):]


def _lin_comb(left, right):
    a1, b1 = left
    a2, b2 = right
    return a1 * a2, a2 * b1 + b2


def rglru_mixer(h, conv_buf, h0, w_in, conv_w, conv_b, wa, ba, wx, bx, lam, w_out):
    n, t, _ = h.shape
    gate_br, x_br = jnp.split(h @ w_in, 2, axis=-1)
    xc, new_buf = causal_conv(x_br, conv_buf, conv_w, conv_b)
    xh = xc.reshape(n, t, N_LRU_HEADS, LRU_BLOCK)
    r = jax.nn.sigmoid((jnp.einsum('bthi,hij->bthj', xh, wa) + ba).astype(jnp.float32)).reshape(n, t, D_RNN)
    ig = jax.nn.sigmoid((jnp.einsum('bthi,hij->bthj', xh, wx) + bx).astype(jnp.float32)).reshape(n, t, D_RNN)
    log_a = -LRU_C * r * jax.nn.softplus(-lam.astype(jnp.float32))
    a = jnp.exp(log_a)
    mult = jnp.sqrt(-jnp.expm1(2.0 * log_a))
    bt = mult * (ig * xc.astype(jnp.float32))
    bt = bt.at[:, 0].add(a[:, 0] * h0.astype(jnp.float32))
    _, hs = lax.associative_scan(_lin_comb, (a, bt), axis=1)
    y = hs.astype(h.dtype) * jax.nn.gelu(gate_br)
    return y @ w_out, new_buf, hs[:, -1].astype(h0.dtype)


def chunk_sgu_mixer(h, w_in, b_in, ln_g, ln_b, ws, bs, w_out):
    n, t, _ = h.shape
    u, v = jnp.split(jax.nn.gelu(h @ w_in + b_in), 2, axis=-1)
    v = layernorm(v, ln_g, ln_b)
    cl = min(t, CHUNK)
    mask = jnp.tril(jnp.ones((cl, cl), dtype=bool))
    w_sp = jnp.where(mask, ws[:, :cl, :cl], 0).astype(v.dtype)
    vc = v.reshape(n, t // cl, cl, N_SG_GROUPS, SG_GROUP)
    mixed = jnp.einsum('gts,bcsgd->bctgd', w_sp, vc) + bs[:, :cl].T[None, None, :, :, None]
    out = u * mixed.reshape(n, t, D_SG)
    return out @ w_out, v


def run_trunk(x, c, conv_state, h_state, ada_w, ada_b, norm_g, lru_w_in, lru_conv_w, lru_conv_b,
              lru_wa, lru_ba, lru_wx, lru_bx, lru_lambda, lru_w_out, sg_w_in, sg_b_in, sg_ln_g,
              sg_ln_b, sg_ws, sg_bs, sg_w_out, mlp_w_up, mlp_w_down):
    convs, hs, vs = [], [], []
    for layer in range(DEPTH):
        sh_m, sc_m, g_m, sh_f, sc_f, g_f = ada_params(c, ada_w[layer], ada_b[layer])
        hin = rmsnorm(x, norm_g[layer, 0]) * (1.0 + sc_m) + sh_m
        j = layer // N_MIXERS
        if layer % N_MIXERS == 0:
            mix, buf, hn = rglru_mixer(hin, conv_state[j], h_state[j], lru_w_in[j], lru_conv_w[j],
                                       lru_conv_b[j], lru_wa[j], lru_ba[j], lru_wx[j], lru_bx[j],
                                       lru_lambda[j], lru_w_out[j])
            convs.append(buf)
            hs.append(hn)
        else:
            mix, v = chunk_sgu_mixer(hin, sg_w_in[j], sg_b_in[j], sg_ln_g[j], sg_ln_b[j],
                                     sg_ws[j], sg_bs[j], sg_w_out[j])
            vs.append(v)
        x = x + g_m * rmsnorm(mix, norm_g[layer, 1])
        hf = rmsnorm(x, norm_g[layer, 2]) * (1.0 + sc_f) + sh_f
        f = jnp.square(jax.nn.relu(hf @ mlp_w_up[layer])) @ mlp_w_down[layer]
        x = x + g_f * rmsnorm(f, norm_g[layer, 3])
    return x, jnp.stack(convs), jnp.stack(hs), jnp.stack(vs)


def setup_inputs(seed: int = 0) -> dict:
    key = jax.random.key(seed)
    ks = iter(jax.random.split(key, 40))
    f32 = jnp.float32

    def nrm(shape, s):
        return jax.random.normal(next(ks), shape, f32) * s

    D = D_MODEL
    NL, NC, H, BS, G = N_LRU_LAYERS, N_SG_LAYERS, N_LRU_HEADS, LRU_BLOCK, N_SG_GROUPS
    a0 = jax.random.uniform(next(ks), (NL, D_RNN), f32, 0.9, 0.999)
    p = a0 ** (1.0 / LRU_C)
    return {
        "x_prompt": nrm((BATCH, SEQ, D), 1.0),
        "x_sample": nrm((DEC_BATCH, DEC_SEQ, D), 1.0),
        "c_prompt": nrm((BATCH, D), 1.0),
        "c_sample": nrm((DEC_BATCH, D), 1.0),
        "state_lru_h": nrm((NL, DEC_BATCH, D_RNN), 0.5),
        "state_lru_conv": nrm((NL, DEC_BATCH, CONV_W - 1, D_RNN), 1.0),
        "ada_w": nrm((DEPTH, D, 6 * D), 0.5 * D ** -0.5),
        "ada_b": nrm((DEPTH, 6 * D), 0.02),
        "norm_g": 1.0 + nrm((DEPTH, 4, D), 0.05),
        "lru_w_in": nrm((NL, D, 2 * D_RNN), D ** -0.5),
        "lru_conv_w": nrm((NL, CONV_W, D_RNN), CONV_W ** -0.5),
        "lru_conv_b": nrm((NL, D_RNN), 0.02),
        "lru_wa": nrm((NL, H, BS, BS), BS ** -0.5),
        "lru_ba": nrm((NL, H, BS), 0.02),
        "lru_wx": nrm((NL, H, BS, BS), BS ** -0.5),
        "lru_bx": nrm((NL, H, BS), 0.02),
        "lru_lambda": jnp.log(p) - jnp.log1p(-p),
        "lru_w_out": nrm((NL, D_RNN, D), D_RNN ** -0.5),
        "sg_w_in": nrm((NC, D, 2 * D_SG), D ** -0.5),
        "sg_b_in": nrm((NC, 2 * D_SG), 0.02),
        "sg_ln_g": 1.0 + nrm((NC, D_SG), 0.05),
        "sg_ln_b": nrm((NC, D_SG), 0.02),
        "sg_ws": nrm((NC, G, CHUNK, CHUNK), CHUNK ** -0.5),
        "sg_bs": 1.0 + nrm((NC, G, CHUNK), 0.1),
        "sg_w_out": nrm((NC, D_SG, D), D_SG ** -0.5),
        "mlp_w_up": nrm((DEPTH, D, D_FF), D ** -0.5),
        "mlp_w_down": nrm((DEPTH, D_FF, D), D_FF ** -0.5),
    }


def reference(x_prompt, x_sample, c_prompt, c_sample, state_lru_h, state_lru_conv, ada_w, ada_b,
              norm_g, lru_w_in, lru_conv_w, lru_conv_b, lru_wa, lru_ba, lru_wx, lru_bx, lru_lambda,
              lru_w_out, sg_w_in, sg_b_in, sg_ln_g, sg_ln_b, sg_ws, sg_bs, sg_w_out, mlp_w_up,
              mlp_w_down):
    b = x_prompt.shape[0]
    zero_conv = jnp.zeros((N_LRU_LAYERS, b, CONV_W - 1, D_RNN), x_prompt.dtype)
    zero_h = jnp.zeros((N_LRU_LAYERS, b, D_RNN), state_lru_h.dtype)
    y_prompt, conv_prompt, h_prompt, _ = run_trunk(
        x_prompt, c_prompt, zero_conv, zero_h, ada_w, ada_b, norm_g, lru_w_in, lru_conv_w,
        lru_conv_b, lru_wa, lru_ba, lru_wx, lru_bx, lru_lambda, lru_w_out, sg_w_in, sg_b_in,
        sg_ln_g, sg_ln_b, sg_ws, sg_bs, sg_w_out, mlp_w_up, mlp_w_down)
    y_sample, conv_sample, h_sample, v_sample = run_trunk(
        x_sample, c_sample, state_lru_conv, state_lru_h, ada_w, ada_b, norm_g, lru_w_in, lru_conv_w,
        lru_conv_b, lru_wa, lru_ba, lru_wx, lru_bx, lru_lambda, lru_w_out, sg_w_in, sg_b_in,
        sg_ln_g, sg_ln_b, sg_ws, sg_bs, sg_w_out, mlp_w_up, mlp_w_down)
    return (y_prompt, y_sample, h_prompt, conv_prompt, h_sample, conv_sample, v_sample)
```

```python
import functools

import jax
import jax.numpy as jnp
from jax import lax
from jax.experimental import pallas as pl
from jax.experimental.pallas import tpu as pltpu

D_MODEL = 1024
D_RNN = 1536
D_SG = 1536
D_FF = 4096
DEPTH = 4
CONV_W = 4
LRU_BLOCK = 96
SG_GROUP = 96
CHUNK = 128
LRU_C = 8.0
EPS = 1e-6

MXU_N = 256
LANES = 128
VMEM_LIMIT_BYTES = 56 * 1024 * 1024

F32 = jnp.float32
BF16 = jnp.bfloat16


def _rmsnorm(x, g):
    return x * lax.rsqrt(jnp.mean(x * x, axis=-1, keepdims=True) + EPS) * g


def _modulate(xn, sc, sh, groups):
    if sc.shape[0] == 1:
        return xn * (1.0 + sc) + sh
    rows, d = xn.shape
    x3 = xn.reshape(groups, rows // groups, d)
    return (x3 * (1.0 + sc)[None] + sh[None]).reshape(rows, d)


def _gated_residual(x, gate, y, groups):
    if gate.shape[0] == 1:
        return x + gate * y
    rows, d = x.shape
    return x + (gate[None] * y.reshape(groups, rows // groups, d)).reshape(rows, d)


def _gelu(x):
    return 0.5 * x * (1.0 + jnp.tanh(0.7978845608028654 * (x + 0.044715 * (x * x * x))))


def _mm(a, w):
    return jnp.dot(a.astype(BF16), w, preferred_element_type=F32)


def _band_ranges(width, block):
    out = []
    for c0 in range(0, width, MXU_N):
        lo = (c0 // block) * block
        hi = ((c0 + MXU_N - 1) // block + 1) * block
        out.append((c0, (lo // LANES) * LANES, min(width, -(-hi // LANES) * LANES)))
    return out


def _block_diag_mm(xb, w_ref, block):
    width = xb.shape[1]
    tiles = [jnp.dot(xb[:, k0:k1], w_ref[k0:k1, c0:c0 + MXU_N], preferred_element_type=F32)
             for c0, k0, k1 in _band_ranges(width, block)]
    return jnp.concatenate(tiles, axis=1)


def _ada_kernel(c_ref, w_ref, b_ref, os_ref, op_ref):
    c = c_ref[...]
    s = c * jax.nn.sigmoid(c)
    m = _mm(s, w_ref[...].astype(BF16)) + b_ref[...]
    n_s = os_ref.shape[0]
    os_ref[...] = m[:n_s]
    op_ref[...] = m[n_s:]


def _ada_call(c_all, ada_w, ada_b, n_s, n_p):
    tn = 1536
    n_all = c_all.shape[0]
    return pl.pallas_call(
        _ada_kernel,
        grid=(DEPTH, 6 * D_MODEL // tn),
        in_specs=[
            pl.BlockSpec((n_all, D_MODEL), lambda l, n: (0, 0)),
            pl.BlockSpec((None, D_MODEL, tn), lambda l, n: (l, 0, n)),
            pl.BlockSpec((None, 1, tn), lambda l, n: (l, 0, n)),
        ],
        out_specs=[
            pl.BlockSpec((None, n_s, tn), lambda l, n: (l, 0, n)),
            pl.BlockSpec((None, n_p, tn), lambda l, n: (l, 0, n)),
        ],
        out_shape=[
            jax.ShapeDtypeStruct((DEPTH, n_s, 6 * D_MODEL), F32),
            jax.ShapeDtypeStruct((DEPTH, n_p, 6 * D_MODEL), F32),
        ],
        compiler_params=pltpu.CompilerParams(
            dimension_semantics=("arbitrary", "arbitrary"), vmem_limit_bytes=VMEM_LIMIT_BYTES),
        name="ada_mod",
    )(c_all, ada_w, ada_b.reshape(DEPTH, 1, 6 * D_MODEL))


def _mlp_kernel(x_ref, sh_ref, sc_ref, g_ref, n2_ref, n3_ref, wup_ref, wdn_ref, o_ref, *, groups):
    x = x_ref[...]
    hf = _modulate(_rmsnorm(x, n2_ref[...]), sc_ref[...], sh_ref[...], groups)
    hid = jnp.square(jnp.maximum(_mm(hf, wup_ref[...]), 0.0))
    f = _mm(hid, wdn_ref[...])
    o_ref[...] = _gated_residual(x, g_ref[...], _rmsnorm(f, n3_ref[...]), groups)


def _resident(shape, index_map):
    return pl.BlockSpec(shape, index_map, pipeline_mode=pl.Buffered(1))


def _mod_specs(mod, layer, first, tiles_per_seq):
    if mod.ndim == 4:
        return [pl.BlockSpec((None, None, 1, D_MODEL),
                             lambda i, k=k: (layer, i // tiles_per_seq, 0, k)) for k in range(first, first + 3)]
    b = mod.shape[1]
    return [pl.BlockSpec((None, b, D_MODEL), lambda i, k=k: (layer, 0, k)) for k in range(first, first + 3)]


def _norm_spec(layer, which):
    return pl.BlockSpec((None, 1, D_MODEL), lambda i: (4 * layer + which, 0, 0))


def _mlp_call(x, mod, norm_g, w_up, w_dn, layer, tm, groups, tiles_per_seq, name):
    rows = x.shape[0]
    row_spec = pl.BlockSpec((tm, D_MODEL), lambda i: (i, 0))
    return pl.pallas_call(
        functools.partial(_mlp_kernel, groups=groups),
        grid=(rows // tm,),
        in_specs=[row_spec, *_mod_specs(mod, layer, 3, tiles_per_seq), _norm_spec(layer, 2), _norm_spec(layer, 3),
                  _resident((None, D_MODEL, D_FF), lambda i: (layer, 0, 0)),
                  _resident((None, D_FF, D_MODEL), lambda i: (layer, 0, 0))],
        out_specs=row_spec,
        out_shape=jax.ShapeDtypeStruct((rows, D_MODEL), F32),
        compiler_params=pltpu.CompilerParams(
            dimension_semantics=("arbitrary",), vmem_limit_bytes=VMEM_LIMIT_BYTES),
        name=name,
    )(x, mod, mod, mod, norm_g, norm_g, w_up, w_dn)


def _lru_kernel(x_ref, sh_ref, sc_ref, g_ref, n0_ref, n1_ref, win_ref, cw_ref, cb_ref, wa_ref, ba_ref,
                wx_ref, bx_ref, lam_ref, wout_ref, conv0_ref, h0_ref,
                o_ref, convo_ref, ho_ref, tail_ref, hc_ref, hs_ref, *, steps, batch):
    tm = steps * batch

    @pl.when(pl.program_id(0) == 0)
    def _():
        tail_ref[...] = conv0_ref[...]
        hc_ref[...] = h0_ref[...]

    x = x_ref[...]
    hin = _modulate(_rmsnorm(x, n0_ref[...]), sc_ref[...], sh_ref[...], steps)
    z = _mm(hin, win_ref[...])
    gate_br = z[:, :D_RNN]
    x_br = z[:, D_RNN:]

    xp = jnp.concatenate([tail_ref[...], x_br], axis=0)
    cw = cw_ref[...]
    xc = cb_ref[...]
    for k in range(CONV_W):
        xc = xc + xp[k * batch:k * batch + tm] * cw[k:k + 1]
    new_tail = xp[tm:]
    tail_ref[...] = new_tail
    convo_ref[...] = new_tail

    xcb = xc.astype(BF16)
    r = jax.nn.sigmoid(_block_diag_mm(xcb, wa_ref, LRU_BLOCK) + ba_ref[...])
    ig = jax.nn.sigmoid(_block_diag_mm(xcb, wx_ref, LRU_BLOCK) + bx_ref[...])
    nlam = -lam_ref[...]
    softplus = jnp.maximum(nlam, 0.0) + jnp.log1p(jnp.exp(-jnp.abs(nlam)))
    log_a = -LRU_C * r * softplus
    a = jnp.exp(log_a)
    mult = jnp.sqrt(-jnp.tanh(log_a) * (a * a + 1.0))
    bt = mult * (ig * xc)

    h = hc_ref[...]
    for t in range(steps):
        h = a[t * batch:(t + 1) * batch] * h + bt[t * batch:(t + 1) * batch]
        hs_ref[t * batch:(t + 1) * batch, :] = h
    hc_ref[...] = h
    ho_ref[...] = h

    y = hs_ref[...] * _gelu(gate_br)
    mix = _mm(y, wout_ref[...])
    o_ref[...] = _gated_residual(x, g_ref[...], _rmsnorm(mix, n1_ref[...]), steps)


def _lru_call(x, mod, norm_g, p, conv0, h0, layer, steps, batch, name):
    j = layer // 2
    tm = steps * batch
    rows = x.shape[0]
    tail_rows = (CONV_W - 1) * batch
    row_spec = pl.BlockSpec((tm, D_MODEL), lambda i: (i, 0))
    vec = lambda: pl.BlockSpec((None, 1, D_RNN), lambda i: (j, 0, 0))
    conv_spec = pl.BlockSpec((tail_rows, D_RNN), lambda i: (0, 0))
    h_spec = pl.BlockSpec((batch, D_RNN), lambda i: (0, 0))
    return pl.pallas_call(
        functools.partial(_lru_kernel, steps=steps, batch=batch),
        grid=(rows // tm,),
        in_specs=[row_spec, *_mod_specs(mod, layer, 0, 1), _norm_spec(layer, 0), _norm_spec(layer, 1),
                  _resident((None, D_MODEL, 2 * D_RNN), lambda i: (j, 0, 0)),
                  pl.BlockSpec((None, CONV_W, D_RNN), lambda i: (j, 0, 0)), vec(),
                  _resident((None, D_RNN, D_RNN), lambda i: (j, 0, 0)), vec(),
                  _resident((None, D_RNN, D_RNN), lambda i: (j, 0, 0)), vec(),
                  vec(),
                  _resident((None, D_RNN, D_MODEL), lambda i: (j, 0, 0)),
                  conv_spec, h_spec],
        out_specs=[row_spec, conv_spec, h_spec],
        out_shape=[jax.ShapeDtypeStruct((rows, D_MODEL), F32),
                   jax.ShapeDtypeStruct((tail_rows, D_RNN), F32),
                   jax.ShapeDtypeStruct((batch, D_RNN), F32)],
        scratch_shapes=[pltpu.VMEM((tail_rows, D_RNN), F32), pltpu.VMEM((batch, D_RNN), F32),
                        pltpu.VMEM((tm, D_RNN), F32)],
        compiler_params=pltpu.CompilerParams(
            dimension_semantics=("arbitrary",), vmem_limit_bytes=VMEM_LIMIT_BYTES),
        name=name,
    )(x, mod, mod, mod, norm_g, norm_g, p["w_in"], p["conv_w"], p["conv_b"], p["wa"], p["ba"],
      p["wx"], p["bx"], p["lam"], p["w_out"], conv0, h0)


def _sgu_front(x_ref, sh_ref, sc_ref, n0_ref, win_ref, bin_ref, lng_ref, lnb_ref, groups):
    x = x_ref[...]
    hin = _modulate(_rmsnorm(x, n0_ref[...]), sc_ref[...], sh_ref[...], groups)
    z = _gelu(_mm(hin, win_ref[...]) + bin_ref[...])
    u = z[:, :D_SG]
    v = z[:, D_SG:]
    vc = v - jnp.mean(v, axis=-1, keepdims=True)
    var = jnp.mean(vc * vc, axis=-1, keepdims=True)
    vn = vc * lax.rsqrt(var + EPS) * lng_ref[...] + lnb_ref[...]
    return x, u, vn


def _sgu_prompt_kernel(x_ref, sh_ref, sc_ref, g_ref, n0_ref, n1_ref, win_ref, bin_ref, lng_ref, lnb_ref,
                       ws_ref, bsx_ref, wout_ref, o_ref, *, chunks):
    x, u, vn = _sgu_front(x_ref, sh_ref, sc_ref, n0_ref, win_ref, bin_ref, lng_ref, lnb_ref, 1)

    t_idx = lax.broadcasted_iota(jnp.int32, (CHUNK, CHUNK), 0)
    s_idx = lax.broadcasted_iota(jnp.int32, (CHUNK, CHUNK), 1)
    causal = s_idx <= t_idx
    n_groups = D_SG // SG_GROUP
    w_sp = [jnp.where(causal, ws_ref[g], 0.0).astype(BF16) for g in range(n_groups)]
    lane = lax.broadcasted_iota(jnp.int32, (1, MXU_N), 1)

    rows = []
    for c in range(chunks):
        tiles = []
        for c0 in range(0, D_SG, MXU_N):
            v_tile = vn[c * CHUNK:(c + 1) * CHUNK, c0:c0 + MXU_N]
            acc = None
            for g in range(c0 // SG_GROUP, (c0 + MXU_N - 1) // SG_GROUP + 1):
                in_group = (lane >= g * SG_GROUP - c0) & (lane < (g + 1) * SG_GROUP - c0)
                part = jnp.dot(w_sp[g], jnp.where(in_group, v_tile, 0.0).astype(BF16),
                               preferred_element_type=F32)
                acc = part if acc is None else acc + part
            tiles.append(acc)
        rows.append(jnp.concatenate(tiles, axis=1) + bsx_ref[...])
    mixed = jnp.concatenate(rows, axis=0)

    mix = _mm(u * mixed, wout_ref[...])
    o_ref[...] = _gated_residual(x, g_ref[...], _rmsnorm(mix, n1_ref[...]), 1)


def _sgu_sample_kernel(x_ref, sh_ref, sc_ref, g_ref, n0_ref, n1_ref, win_ref, bin_ref, lng_ref, lnb_ref,
                       coef_ref, bias_ref, wout_ref, o_ref, v_ref, *, steps, batch):
    x, u, vn = _sgu_front(x_ref, sh_ref, sc_ref, n0_ref, win_ref, bin_ref, lng_ref, lnb_ref, steps)
    v_ref[...] = vn
    slabs = []
    for t in range(steps):
        acc = None
        for s in range(t + 1):
            term = coef_ref[t * steps + s:t * steps + s + 1, :] * vn[s * batch:(s + 1) * batch]
            acc = term if acc is None else acc + term
        slabs.append(acc + bias_ref[t:t + 1, :])
    mixed = jnp.concatenate(slabs, axis=0)
    mix = _mm(u * mixed, wout_ref[...])
    o_ref[...] = _gated_residual(x, g_ref[...], _rmsnorm(mix, n1_ref[...]), steps)


def _sgu_common_specs(mod, layer, tiles_per_seq):
    j = layer // 2
    return [*_mod_specs(mod, layer, 0, tiles_per_seq), _norm_spec(layer, 0), _norm_spec(layer, 1),
            _resident((None, D_MODEL, 2 * D_SG), lambda i: (j, 0, 0)),
            pl.BlockSpec((None, 1, 2 * D_SG), lambda i: (j, 0, 0)),
            pl.BlockSpec((None, 1, D_SG), lambda i: (j, 0, 0)),
            pl.BlockSpec((None, 1, D_SG), lambda i: (j, 0, 0))]


def _sgu_prompt_call(x, mod, norm_g, p, layer, tm, seq_len, name):
    j = layer // 2
    rows = x.shape[0]
    row_spec = pl.BlockSpec((tm, D_MODEL), lambda i: (i, 0))
    n_groups = D_SG // SG_GROUP
    return pl.pallas_call(
        functools.partial(_sgu_prompt_kernel, chunks=tm // CHUNK),
        grid=(rows // tm,),
        in_specs=[row_spec, *_sgu_common_specs(mod, layer, seq_len // tm),
                  pl.BlockSpec((None, n_groups, CHUNK, CHUNK), lambda i: (j, 0, 0, 0)),
                  pl.BlockSpec((None, CHUNK, D_SG), lambda i: (j, 0, 0)),
                  _resident((None, D_SG, D_MODEL), lambda i: (j, 0, 0))],
        out_specs=row_spec,
        out_shape=jax.ShapeDtypeStruct((rows, D_MODEL), F32),
        compiler_params=pltpu.CompilerParams(
            dimension_semantics=("arbitrary",), vmem_limit_bytes=VMEM_LIMIT_BYTES),
        name=name,
    )(x, mod, mod, mod, norm_g, norm_g, p["w_in"], p["b_in"], p["ln_g"], p["ln_b"], p["ws"], p["bsx"], p["w_out"])


def _sgu_sample_call(x, mod, norm_g, p, layer, steps, batch, name):
    j = layer // 2
    rows = x.shape[0]
    row_spec = pl.BlockSpec((rows, D_MODEL), lambda i: (0, 0))
    v_spec = pl.BlockSpec((rows, D_SG), lambda i: (0, 0))
    return pl.pallas_call(
        functools.partial(_sgu_sample_kernel, steps=steps, batch=batch),
        grid=(1,),
        in_specs=[row_spec, *_sgu_common_specs(mod, layer, 1),
                  pl.BlockSpec((None, steps * steps, D_SG), lambda i: (j, 0, 0)),
                  pl.BlockSpec((None, steps, D_SG), lambda i: (j, 0, 0)),
                  _resident((None, D_SG, D_MODEL), lambda i: (j, 0, 0))],
        out_specs=[row_spec, v_spec],
        out_shape=[jax.ShapeDtypeStruct((rows, D_MODEL), F32), jax.ShapeDtypeStruct((rows, D_SG), F32)],
        compiler_params=pltpu.CompilerParams(
            dimension_semantics=("arbitrary",), vmem_limit_bytes=VMEM_LIMIT_BYTES),
        name=name,
    )(x, mod, mod, mod, norm_g, norm_g, p["w_in"], p["b_in"], p["ln_g"], p["ln_b"], p["coef"], p["bias"],
      p["w_out"])


def _block_diag(w):
    n_l, n_h, b, _ = w.shape
    eye = jnp.eye(n_h, dtype=w.dtype)
    return (w[:, :, :, None, :] * eye[None, :, None, :, None]).reshape(n_l, n_h * b, n_h * b)


def _to_time_major(x):
    n, t, d = x.shape
    return x.transpose(1, 0, 2).reshape(t * n, d)


def _from_time_major(x, n):
    rows, d = x.shape
    return x.reshape(rows // n, n, d).transpose(1, 0, 2)


def kernel(x_prompt, x_sample, c_prompt, c_sample, state_lru_h, state_lru_conv, ada_w, ada_b, norm_g, lru_w_in, lru_conv_w, lru_conv_b, lru_wa, lru_ba, lru_wx, lru_bx, lru_lambda, lru_w_out, sg_w_in, sg_b_in, sg_ln_g, sg_ln_b, sg_ws, sg_bs, sg_w_out, mlp_w_up, mlp_w_down):
    n_p, t_p, _ = x_prompt.shape
    n_s, t_s, _ = x_sample.shape
    n_lru = lru_w_in.shape[0]
    n_sg = sg_w_in.shape[0]

    mod_s, mod_p = _ada_call(jnp.concatenate([c_sample, c_prompt], axis=0), ada_w, ada_b, n_s, n_p)
    mod_p_seq = mod_p.reshape(DEPTH, n_p, 1, 6 * D_MODEL)
    norms = norm_g.reshape(DEPTH * 4, 1, D_MODEL)

    lru = dict(
        w_in=lru_w_in.astype(BF16), conv_w=lru_conv_w, conv_b=lru_conv_b.reshape(n_lru, 1, D_RNN),
        wa=_block_diag(lru_wa).astype(BF16), ba=lru_ba.reshape(n_lru, 1, D_RNN),
        wx=_block_diag(lru_wx).astype(BF16), bx=lru_bx.reshape(n_lru, 1, D_RNN),
        lam=lru_lambda.reshape(n_lru, 1, D_RNN), w_out=lru_w_out.astype(BF16))
    sgu = dict(
        w_in=sg_w_in.astype(BF16), b_in=sg_b_in.reshape(n_sg, 1, 2 * D_SG),
        ln_g=sg_ln_g.reshape(n_sg, 1, D_SG), ln_b=sg_ln_b.reshape(n_sg, 1, D_SG), w_out=sg_w_out.astype(BF16),
        ws=sg_ws,
        bsx=jnp.repeat(sg_bs.transpose(0, 2, 1), SG_GROUP, axis=2),
        coef=jnp.repeat(sg_ws[:, :, :t_s, :t_s].transpose(0, 2, 3, 1), SG_GROUP, axis=3).reshape(n_sg, t_s * t_s, D_SG),
        bias=jnp.repeat(sg_bs[:, :, :t_s].transpose(0, 2, 1), SG_GROUP, axis=2))
    w_up = mlp_w_up.astype(BF16)
    w_dn = mlp_w_down.astype(BF16)

    tm = 256
    steps_p = tm // n_p
    xp = x_prompt.reshape(n_p * t_p, D_MODEL)
    xs = _to_time_major(x_sample)
    zero_conv = jnp.zeros(((CONV_W - 1) * n_p, D_RNN), F32)
    zero_h = jnp.zeros((n_p, D_RNN), F32)
    h_p, conv_p, h_s, conv_s, v_s = [], [], [], [], []
    for layer in range(DEPTH):
        j = layer // 2
        if layer % 2 == 0:
            xp = _to_time_major(xp.reshape(n_p, t_p, D_MODEL))
            xp, cp, hp = _lru_call(xp, mod_p, norms, lru, zero_conv, zero_h, layer, steps_p, n_p, f"lru_prompt_{layer}")
            xp = _mlp_call(xp, mod_p, norms, w_up, w_dn, layer, tm, steps_p, 1, f"mlp_prompt_{layer}")
            xp = _from_time_major(xp, n_p).reshape(n_p * t_p, D_MODEL)
            xs, cs, hs = _lru_call(xs, mod_s, norms, lru, _to_time_major(state_lru_conv[j]), state_lru_h[j],
                                   layer, t_s, n_s, f"lru_sample_{layer}")
            h_p.append(hp)
            conv_p.append(_from_time_major(cp, n_p))
            h_s.append(hs)
            conv_s.append(_from_time_major(cs, n_s))
        else:
            xp = _sgu_prompt_call(xp, mod_p_seq, norms, sgu, layer, tm, t_p, f"sgu_prompt_{layer}")
            xp = _mlp_call(xp, mod_p_seq, norms, w_up, w_dn, layer, tm, 1, t_p // tm, f"mlp_prompt_{layer}")
            xs, v = _sgu_sample_call(xs, mod_s, norms, sgu, layer, t_s, n_s, f"sgu_sample_{layer}")
            v_s.append(_from_time_major(v, n_s))
        xs = _mlp_call(xs, mod_s, norms, w_up, w_dn, layer, t_s * n_s, t_s, 1, f"mlp_sample_{layer}")

    y_prompt = xp.reshape(n_p, t_p, D_MODEL)
    y_sample = _from_time_major(xs, n_s)
    return (y_prompt, y_sample, jnp.stack(h_p), jnp.stack(conv_p), jnp.stack(h_s), jnp.stack(conv_s),
            jnp.stack(v_s))
```

```python
import functools

import jax
import jax.numpy as jnp
from jax import lax
from jax.experimental import pallas as pl
from jax.experimental.pallas import tpu as pltpu

D_MODEL = 1024
D_RNN = 1536
D_SG = 1536
D_FF = 4096
DEPTH = 4
CONV_W = 4
LRU_BLOCK = 96
SG_GROUP = 96
CHUNK = 128
LRU_C = 8.0
EPS = 1e-6

MXU_N = 256
LANES = 128
VMEM_LIMIT_BYTES = 56 * 1024 * 1024

F32 = jnp.float32
BF16 = jnp.bfloat16


def _row_rsqrt_ms(x):
    return lax.rsqrt(jnp.mean(x * x, axis=-1, keepdims=True) + EPS)


def _slabs(fn, x, vecs, groups):
    if vecs[0].shape[0] == 1:
        return fn(x, *vecs)
    rows, d = x.shape
    return fn(x.reshape(groups, rows // groups, d), *[v[None] for v in vecs]).reshape(rows, d)


def _norm_modulate(x, g, sc, sh, groups):
    return _slabs(lambda xs, scale, shift: xs * scale + shift, x * _row_rsqrt_ms(x), (g * (1.0 + sc), sh), groups)


def _norm_gated_residual(x, gate, y, g, groups):
    yn = _slabs(lambda ys, scale: ys * scale, y * _row_rsqrt_ms(y), (gate * g,), groups)
    return x + yn


def _gelu(x):
    k = 0.7978845608028654
    t = jnp.tanh(x * (k + (k * 0.044715) * (x * x)))
    return x * (0.5 + 0.5 * t)


def _mm(a, w):
    return jnp.dot(a.astype(BF16), w, preferred_element_type=F32)


def _band_ranges(width, block):
    out = []
    for c0 in range(0, width, MXU_N):
        lo = (c0 // block) * block
        hi = ((c0 + MXU_N - 1) // block + 1) * block
        out.append((c0, (lo // LANES) * LANES, min(width, -(-hi // LANES) * LANES)))
    return out


def _block_diag_mm(xb, w_ref, block):
    width = xb.shape[1]
    tiles = [jnp.dot(xb[:, k0:k1], w_ref[k0:k1, c0:c0 + MXU_N], preferred_element_type=F32)
             for c0, k0, k1 in _band_ranges(width, block)]
    return jnp.concatenate(tiles, axis=1)


def _ada_kernel(c_ref, w_ref, b_ref, os_ref, op_ref):
    c = c_ref[...]
    s = c * jax.nn.sigmoid(c)
    m = _mm(s, w_ref[...].astype(BF16)) + b_ref[...]
    n_s = os_ref.shape[0]
    os_ref[...] = m[:n_s]
    op_ref[...] = m[n_s:]


def _ada_call(c_all, ada_w, ada_b, n_s, n_p):
    tn = 1536
    n_all = c_all.shape[0]
    return pl.pallas_call(
        _ada_kernel,
        grid=(DEPTH, 6 * D_MODEL // tn),
        in_specs=[
            pl.BlockSpec((n_all, D_MODEL), lambda l, n: (0, 0)),
            pl.BlockSpec((None, D_MODEL, tn), lambda l, n: (l, 0, n)),
            pl.BlockSpec((None, 1, tn), lambda l, n: (l, 0, n)),
        ],
        out_specs=[
            pl.BlockSpec((None, n_s, tn), lambda l, n: (l, 0, n)),
            pl.BlockSpec((None, n_p, tn), lambda l, n: (l, 0, n)),
        ],
        out_shape=[
            jax.ShapeDtypeStruct((DEPTH, n_s, 6 * D_MODEL), F32),
            jax.ShapeDtypeStruct((DEPTH, n_p, 6 * D_MODEL), F32),
        ],
        compiler_params=pltpu.CompilerParams(
            dimension_semantics=("arbitrary", "arbitrary"), vmem_limit_bytes=VMEM_LIMIT_BYTES),
        name="ada_mod",
    )(c_all, ada_w, ada_b.reshape(DEPTH, 1, 6 * D_MODEL))


def _mlp_kernel(x_ref, sh_ref, sc_ref, g_ref, n2_ref, n3_ref, wup_ref, wdn_ref, o_ref, *, groups, subtiles):
    rows = x_ref.shape[0] // subtiles
    for s in range(subtiles):
        x = x_ref[s * rows:(s + 1) * rows, :]
        hf = _norm_modulate(x, n2_ref[...], sc_ref[...], sh_ref[...], groups // subtiles)
        hid = jnp.square(jnp.maximum(_mm(hf, wup_ref[...]), 0.0))
        f = _mm(hid, wdn_ref[...])
        o_ref[s * rows:(s + 1) * rows, :] = _norm_gated_residual(x, g_ref[...], f, n3_ref[...], groups // subtiles)


def _resident(shape, index_map):
    return pl.BlockSpec(shape, index_map, pipeline_mode=pl.Buffered(1))


def _mod_specs(mod, layer, first, tiles_per_seq):
    if mod.ndim == 4:
        return [pl.BlockSpec((None, None, 1, D_MODEL),
                             lambda i, k=k: (layer, i // tiles_per_seq, 0, k)) for k in range(first, first + 3)]
    b = mod.shape[1]
    return [pl.BlockSpec((None, b, D_MODEL), lambda i, k=k: (layer, 0, k)) for k in range(first, first + 3)]


def _norm_spec(layer, which):
    return pl.BlockSpec((None, 1, D_MODEL), lambda i: (4 * layer + which, 0, 0))


def _mlp_call(x, mod, norm_g, w_up, w_dn, layer, tm, groups, tiles_per_seq, name, subtiles=2):
    rows = x.shape[0]
    row_spec = pl.BlockSpec((tm, D_MODEL), lambda i: (i, 0))
    return pl.pallas_call(
        functools.partial(_mlp_kernel, groups=groups, subtiles=subtiles),
        grid=(rows // tm,),
        in_specs=[row_spec, *_mod_specs(mod, layer, 3, tiles_per_seq), _norm_spec(layer, 2), _norm_spec(layer, 3),
                  _resident((None, D_MODEL, D_FF), lambda i: (layer, 0, 0)),
                  _resident((None, D_FF, D_MODEL), lambda i: (layer, 0, 0))],
        out_specs=row_spec,
        out_shape=jax.ShapeDtypeStruct((rows, D_MODEL), F32),
        compiler_params=pltpu.CompilerParams(
            dimension_semantics=("arbitrary",), vmem_limit_bytes=VMEM_LIMIT_BYTES),
        name=name,
    )(x, mod, mod, mod, norm_g, norm_g, w_up, w_dn)


def _lru_kernel(x_ref, sh_ref, sc_ref, g_ref, n0_ref, n1_ref, win_ref, cw_ref, cb_ref, wa_ref, ba_ref,
                wx_ref, bx_ref, lam_ref, wout_ref, conv0_ref, h0_ref,
                o_ref, convo_ref, ho_ref, tail_ref, hc_ref, hs_ref, *, steps, batch, subtiles):
    @pl.when(pl.program_id(0) == 0)
    def _():
        tail_ref[...] = conv0_ref[...]
        hc_ref[...] = h0_ref[...]

    nlam = -lam_ref[...]
    neg_c_softplus = -LRU_C * (jnp.maximum(nlam, 0.0) + jnp.log1p(jnp.exp(-jnp.abs(nlam))))
    cw = cw_ref[...]
    tail = tail_ref[...]
    h = hc_ref[...]
    sub_steps = steps // subtiles
    rows = sub_steps * batch
    for s in range(subtiles):
        r0 = s * rows
        x = x_ref[r0:r0 + rows, :]
        hin = _norm_modulate(x, n0_ref[...], sc_ref[...], sh_ref[...], sub_steps)
        z = _mm(hin, win_ref[...])
        gate_br = z[:, :D_RNN]
        x_br = z[:, D_RNN:]

        xp = jnp.concatenate([tail, x_br], axis=0)
        xc = cb_ref[...]
        for k in range(CONV_W):
            xc = xc + xp[k * batch:k * batch + rows] * cw[k:k + 1]
        tail = xp[rows:]

        xcb = xc.astype(BF16)
        r = jax.nn.sigmoid(_block_diag_mm(xcb, wa_ref, LRU_BLOCK) + ba_ref[...])
        ig = jax.nn.sigmoid(_block_diag_mm(xcb, wx_ref, LRU_BLOCK) + bx_ref[...])
        log_a = r * neg_c_softplus
        a = jnp.exp(log_a)
        q = jnp.tanh(log_a) * (-1.0 - a * a)
        mult = jnp.where(q > 0.0, q * lax.rsqrt(q), 0.0)
        bt = mult * (ig * xc)

        for t in range(sub_steps):
            h = a[t * batch:(t + 1) * batch] * h + bt[t * batch:(t + 1) * batch]
            hs_ref[r0 + t * batch:r0 + (t + 1) * batch, :] = h

        y = hs_ref[r0:r0 + rows, :] * _gelu(gate_br)
        mix = _mm(y, wout_ref[...])
        o_ref[r0:r0 + rows, :] = _norm_gated_residual(x, g_ref[...], mix, n1_ref[...], sub_steps)

    tail_ref[...] = tail
    convo_ref[...] = tail
    hc_ref[...] = h
    ho_ref[...] = h


def _lru_call(x, mod, norm_g, p, conv0, h0, layer, steps, batch, name, subtiles=2):
    j = layer // 2
    tm = steps * batch
    rows = x.shape[0]
    tail_rows = (CONV_W - 1) * batch
    row_spec = pl.BlockSpec((tm, D_MODEL), lambda i: (i, 0))
    vec = lambda: pl.BlockSpec((None, 1, D_RNN), lambda i: (j, 0, 0))
    conv_spec = pl.BlockSpec((tail_rows, D_RNN), lambda i: (0, 0))
    h_spec = pl.BlockSpec((batch, D_RNN), lambda i: (0, 0))
    return pl.pallas_call(
        functools.partial(_lru_kernel, steps=steps, batch=batch, subtiles=subtiles),
        grid=(rows // tm,),
        in_specs=[row_spec, *_mod_specs(mod, layer, 0, 1), _norm_spec(layer, 0), _norm_spec(layer, 1),
                  _resident((None, D_MODEL, 2 * D_RNN), lambda i: (j, 0, 0)),
                  pl.BlockSpec((None, CONV_W, D_RNN), lambda i: (j, 0, 0)), vec(),
                  _resident((None, D_RNN, D_RNN), lambda i: (j, 0, 0)), vec(),
                  _resident((None, D_RNN, D_RNN), lambda i: (j, 0, 0)), vec(),
                  vec(),
                  _resident((None, D_RNN, D_MODEL), lambda i: (j, 0, 0)),
                  conv_spec, h_spec],
        out_specs=[row_spec, conv_spec, h_spec],
        out_shape=[jax.ShapeDtypeStruct((rows, D_MODEL), F32),
                   jax.ShapeDtypeStruct((tail_rows, D_RNN), F32),
                   jax.ShapeDtypeStruct((batch, D_RNN), F32)],
        scratch_shapes=[pltpu.VMEM((tail_rows, D_RNN), F32), pltpu.VMEM((batch, D_RNN), F32),
                        pltpu.VMEM((tm, D_RNN), F32)],
        compiler_params=pltpu.CompilerParams(
            dimension_semantics=("arbitrary",), vmem_limit_bytes=VMEM_LIMIT_BYTES),
        name=name,
    )(x, mod, mod, mod, norm_g, norm_g, p["w_in"], p["conv_w"], p["conv_b"], p["wa"], p["ba"],
      p["wx"], p["bx"], p["lam"], p["w_out"], conv0, h0)


def _sgu_front(x, sh_ref, sc_ref, n0_ref, win_ref, bin_ref, lng_ref, lnb_ref, groups):
    hin = _norm_modulate(x, n0_ref[...], sc_ref[...], sh_ref[...], groups)
    z = _gelu(_mm(hin, win_ref[...]) + bin_ref[...])
    u = z[:, :D_SG]
    v = z[:, D_SG:]
    vc = v - jnp.mean(v, axis=-1, keepdims=True)
    var = jnp.mean(vc * vc, axis=-1, keepdims=True)
    vn = vc * lax.rsqrt(var + EPS) * lng_ref[...] + lnb_ref[...]
    return u, vn


def _sgu_prompt_kernel(x_ref, sh_ref, sc_ref, g_ref, n0_ref, n1_ref, win_ref, bin_ref, lng_ref, lnb_ref,
                       ws_ref, bsx_ref, wout_ref, o_ref, *, chunks, subtiles):
    t_idx = lax.broadcasted_iota(jnp.int32, (CHUNK, CHUNK), 0)
    s_idx = lax.broadcasted_iota(jnp.int32, (CHUNK, CHUNK), 1)
    causal = s_idx <= t_idx
    n_groups = D_SG // SG_GROUP
    w_sp = [jnp.where(causal, ws_ref[g], 0.0).astype(BF16) for g in range(n_groups)]
    lane = lax.broadcasted_iota(jnp.int32, (1, MXU_N), 1)

    sub_chunks = chunks // subtiles
    rows = sub_chunks * CHUNK
    for s in range(subtiles):
        x = x_ref[s * rows:(s + 1) * rows, :]
        u, vn = _sgu_front(x, sh_ref, sc_ref, n0_ref, win_ref, bin_ref, lng_ref, lnb_ref, 1)
        mixed_rows = []
        for c in range(sub_chunks):
            tiles = []
            for c0 in range(0, D_SG, MXU_N):
                v_tile = vn[c * CHUNK:(c + 1) * CHUNK, c0:c0 + MXU_N]
                acc = None
                for g in range(c0 // SG_GROUP, (c0 + MXU_N - 1) // SG_GROUP + 1):
                    in_group = (lane >= g * SG_GROUP - c0) & (lane < (g + 1) * SG_GROUP - c0)
                    part = jnp.dot(w_sp[g], jnp.where(in_group, v_tile, 0.0).astype(BF16),
                                   preferred_element_type=F32)
                    acc = part if acc is None else acc + part
                tiles.append(acc)
            mixed_rows.append(jnp.concatenate(tiles, axis=1) + bsx_ref[...])
        mixed = jnp.concatenate(mixed_rows, axis=0)
        mix = _mm(u * mixed, wout_ref[...])
        o_ref[s * rows:(s + 1) * rows, :] = _norm_gated_residual(x, g_ref[...], mix, n1_ref[...], 1)


def _sgu_sample_kernel(x_ref, sh_ref, sc_ref, g_ref, n0_ref, n1_ref, win_ref, bin_ref, lng_ref, lnb_ref,
                       coef_ref, bias_ref, wout_ref, o_ref, v_ref, *, steps, batch):
    x = x_ref[...]
    u, vn = _sgu_front(x, sh_ref, sc_ref, n0_ref, win_ref, bin_ref, lng_ref, lnb_ref, steps)
    v_ref[...] = vn
    slabs = []
    for t in range(steps):
        acc = None
        for s in range(t + 1):
            term = coef_ref[t * steps + s:t * steps + s + 1, :] * vn[s * batch:(s + 1) * batch]
            acc = term if acc is None else acc + term
        slabs.append(acc + bias_ref[t:t + 1, :])
    mixed = jnp.concatenate(slabs, axis=0)
    mix = _mm(u * mixed, wout_ref[...])
    o_ref[...] = _norm_gated_residual(x, g_ref[...], mix, n1_ref[...], steps)


def _sgu_common_specs(mod, layer, tiles_per_seq):
    j = layer // 2
    return [*_mod_specs(mod, layer, 0, tiles_per_seq), _norm_spec(layer, 0), _norm_spec(layer, 1),
            _resident((None, D_MODEL, 2 * D_SG), lambda i: (j, 0, 0)),
            pl.BlockSpec((None, 1, 2 * D_SG), lambda i: (j, 0, 0)),
            pl.BlockSpec((None, 1, D_SG), lambda i: (j, 0, 0)),
            pl.BlockSpec((None, 1, D_SG), lambda i: (j, 0, 0))]


def _sgu_prompt_call(x, mod, norm_g, p, layer, tm, seq_len, name, subtiles=2):
    j = layer // 2
    rows = x.shape[0]
    row_spec = pl.BlockSpec((tm, D_MODEL), lambda i: (i, 0))
    n_groups = D_SG // SG_GROUP
    return pl.pallas_call(
        functools.partial(_sgu_prompt_kernel, chunks=tm // CHUNK, subtiles=subtiles),
        grid=(rows // tm,),
        in_specs=[row_spec, *_sgu_common_specs(mod, layer, seq_len // tm),
                  pl.BlockSpec((None, n_groups, CHUNK, CHUNK), lambda i: (j, 0, 0, 0)),
                  pl.BlockSpec((None, CHUNK, D_SG), lambda i: (j, 0, 0)),
                  _resident((None, D_SG, D_MODEL), lambda i: (j, 0, 0))],
        out_specs=row_spec,
        out_shape=jax.ShapeDtypeStruct((rows, D_MODEL), F32),
        compiler_params=pltpu.CompilerParams(
            dimension_semantics=("arbitrary",), vmem_limit_bytes=VMEM_LIMIT_BYTES),
        name=name,
    )(x, mod, mod, mod, norm_g, norm_g, p["w_in"], p["b_in"], p["ln_g"], p["ln_b"], p["ws"], p["bsx"], p["w_out"])


def _sgu_sample_call(x, mod, norm_g, p, layer, steps, batch, name):
    j = layer // 2
    rows = x.shape[0]
    row_spec = pl.BlockSpec((rows, D_MODEL), lambda i: (0, 0))
    v_spec = pl.BlockSpec((rows, D_SG), lambda i: (0, 0))
    return pl.pallas_call(
        functools.partial(_sgu_sample_kernel, steps=steps, batch=batch),
        grid=(1,),
        in_specs=[row_spec, *_sgu_common_specs(mod, layer, 1),
                  pl.BlockSpec((None, steps * steps, D_SG), lambda i: (j, 0, 0)),
                  pl.BlockSpec((None, steps, D_SG), lambda i: (j, 0, 0)),
                  _resident((None, D_SG, D_MODEL), lambda i: (j, 0, 0))],
        out_specs=[row_spec, v_spec],
        out_shape=[jax.ShapeDtypeStruct((rows, D_MODEL), F32), jax.ShapeDtypeStruct((rows, D_SG), F32)],
        compiler_params=pltpu.CompilerParams(
            dimension_semantics=("arbitrary",), vmem_limit_bytes=VMEM_LIMIT_BYTES),
        name=name,
    )(x, mod, mod, mod, norm_g, norm_g, p["w_in"], p["b_in"], p["ln_g"], p["ln_b"], p["coef"], p["bias"],
      p["w_out"])


def _block_diag(w):
    n_l, n_h, b, _ = w.shape
    eye = jnp.eye(n_h, dtype=w.dtype)
    return (w[:, :, :, None, :] * eye[None, :, None, :, None]).reshape(n_l, n_h * b, n_h * b)


def _to_time_major(x):
    n, t, d = x.shape
    return x.transpose(1, 0, 2).reshape(t * n, d)


def _from_time_major(x, n):
    rows, d = x.shape
    return x.reshape(rows // n, n, d).transpose(1, 0, 2)


def kernel(x_prompt, x_sample, c_prompt, c_sample, state_lru_h, state_lru_conv, ada_w, ada_b, norm_g, lru_w_in, lru_conv_w, lru_conv_b, lru_wa, lru_ba, lru_wx, lru_bx, lru_lambda, lru_w_out, sg_w_in, sg_b_in, sg_ln_g, sg_ln_b, sg_ws, sg_bs, sg_w_out, mlp_w_up, mlp_w_down):
    n_p, t_p, _ = x_prompt.shape
    n_s, t_s, _ = x_sample.shape
    n_lru = lru_w_in.shape[0]
    n_sg = sg_w_in.shape[0]

    mod_s, mod_p = _ada_call(jnp.concatenate([c_sample, c_prompt], axis=0), ada_w, ada_b, n_s, n_p)
    mod_p_seq = mod_p.reshape(DEPTH, n_p, 1, 6 * D_MODEL)
    norms = norm_g.reshape(DEPTH * 4, 1, D_MODEL)

    lru = dict(
        w_in=lru_w_in.astype(BF16), conv_w=lru_conv_w, conv_b=lru_conv_b.reshape(n_lru, 1, D_RNN),
        wa=_block_diag(lru_wa).astype(BF16), ba=lru_ba.reshape(n_lru, 1, D_RNN),
        wx=_block_diag(lru_wx).astype(BF16), bx=lru_bx.reshape(n_lru, 1, D_RNN),
        lam=lru_lambda.reshape(n_lru, 1, D_RNN), w_out=lru_w_out.astype(BF16))
    sgu = dict(
        w_in=sg_w_in.astype(BF16), b_in=sg_b_in.reshape(n_sg, 1, 2 * D_SG),
        ln_g=sg_ln_g.reshape(n_sg, 1, D_SG), ln_b=sg_ln_b.reshape(n_sg, 1, D_SG), w_out=sg_w_out.astype(BF16),
        ws=sg_ws,
        bsx=jnp.repeat(sg_bs.transpose(0, 2, 1), SG_GROUP, axis=2),
        coef=jnp.repeat(sg_ws[:, :, :t_s, :t_s].transpose(0, 2, 3, 1), SG_GROUP, axis=3).reshape(n_sg, t_s * t_s, D_SG),
        bias=jnp.repeat(sg_bs[:, :, :t_s].transpose(0, 2, 1), SG_GROUP, axis=2))
    w_up = mlp_w_up.astype(BF16)
    w_dn = mlp_w_down.astype(BF16)

    tm = 512
    steps_p = tm // n_p
    xp = x_prompt.reshape(n_p * t_p, D_MODEL)
    xs = _to_time_major(x_sample)
    zero_conv = jnp.zeros(((CONV_W - 1) * n_p, D_RNN), F32)
    zero_h = jnp.zeros((n_p, D_RNN), F32)
    h_p, conv_p, h_s, conv_s, v_s = [], [], [], [], []
    for layer in range(DEPTH):
        j = layer // 2
        if layer % 2 == 0:
            xp = _to_time_major(xp.reshape(n_p, t_p, D_MODEL))
            xp, cp, hp = _lru_call(xp, mod_p, norms, lru, zero_conv, zero_h, layer, steps_p, n_p, f"lru_prompt_{layer}")
            xp = _mlp_call(xp, mod_p, norms, w_up, w_dn, layer, tm, steps_p, 1, f"mlp_prompt_{layer}")
            xp = _from_time_major(xp, n_p).reshape(n_p * t_p, D_MODEL)
            xs, cs, hs = _lru_call(xs, mod_s, norms, lru, _to_time_major(state_lru_conv[j]), state_lru_h[j],
                                   layer, t_s, n_s, f"lru_sample_{layer}")
            h_p.append(hp)
            conv_p.append(_from_time_major(cp, n_p))
            h_s.append(hs)
            conv_s.append(_from_time_major(cs, n_s))
        else:
            xp = _sgu_prompt_call(xp, mod_p_seq, norms, sgu, layer, tm, t_p, f"sgu_prompt_{layer}")
            xp = _mlp_call(xp, mod_p_seq, norms, w_up, w_dn, layer, tm, 1, t_p // tm, f"mlp_prompt_{layer}")
            xs, v = _sgu_sample_call(xs, mod_s, norms, sgu, layer, t_s, n_s, f"sgu_sample_{layer}")
            v_s.append(_from_time_major(v, n_s))
        xs = _mlp_call(xs, mod_s, norms, w_up, w_dn, layer, t_s * n_s, t_s, 1, f"mlp_sample_{layer}")

    y_prompt = xp.reshape(n_p, t_p, D_MODEL)
    y_sample = _from_time_major(xs, n_s)
    return (y_prompt, y_sample, jnp.stack(h_p), jnp.stack(conv_p), jnp.stack(h_s), jnp.stack(conv_s),
            jnp.stack(v_s))
```

```python
import functools

import jax
import jax.numpy as jnp
from jax import lax
from jax.experimental import pallas as pl
from jax.experimental.pallas import tpu as pltpu

D_MODEL = 1024
D_RNN = 1536
D_SG = 1536
D_FF = 4096
DEPTH = 4
CONV_W = 4
LRU_BLOCK = 96
SG_GROUP = 96
CHUNK = 128
LRU_C = 8.0
EPS = 1e-6

MXU_N = 256
LANES = 128
BAND_K = 512
VMEM_LIMIT_BYTES = 56 * 1024 * 1024
CAST_STEPS = 8

F32 = jnp.float32
BF16 = jnp.bfloat16


def _row_rsqrt_ms(x):
    return lax.rsqrt(jnp.mean(x * x, axis=-1, keepdims=True) + EPS)


def _slabs(fn, x, vecs, groups):
    if vecs[0].shape[0] == 1:
        return fn(x, *vecs)
    rows, d = x.shape
    return fn(x.reshape(groups, rows // groups, d), *[v[None] for v in vecs]).reshape(rows, d)


def _norm_modulate(x, g, sc, sh, groups):
    return _slabs(lambda xs, scale, shift: xs * scale + shift, x * _row_rsqrt_ms(x), (g * (1.0 + sc), sh), groups)


def _norm_gated_residual(x, gate, y, g, groups):
    yn = _slabs(lambda ys, scale: ys * scale, y * _row_rsqrt_ms(y), (gate * g,), groups)
    return x + yn


def _gelu(x):
    k = 0.7978845608028654
    t = jnp.tanh(x * (k + (k * 0.044715) * (x * x)))
    return x * (0.5 + 0.5 * t)


def _mm(a, w):
    return jnp.dot(a.astype(BF16), w, preferred_element_type=F32)


def _band_starts(width, block):
    out = []
    for c0 in range(0, width, MXU_N):
        lo = (c0 // block) * block
        hi = ((c0 + MXU_N - 1) // block + 1) * block
        k0 = min((lo // LANES) * LANES, width - BAND_K)
        assert k0 <= lo and hi <= k0 + BAND_K
        out.append((c0, k0))
    return out


def _block_diag_mm(xb, w_ref, block):
    tiles = [jnp.dot(xb[:, k0:k0 + BAND_K], w_ref[t], preferred_element_type=F32)
             for t, (_, k0) in enumerate(_band_starts(xb.shape[1], block))]
    return jnp.concatenate(tiles, axis=1)


def _cast_weight_chunks(step, srcs, outs, dsts):
    for src, out, dst in zip(srcs, outs, dsts):
        rows = src.shape[0]
        wb = src[...].astype(BF16)
        out[...] = wb
        dst[pl.ds(pl.multiple_of(step * rows, rows), rows), :] = wb


def _resident(shape, index_map):
    return pl.BlockSpec(shape, index_map, pipeline_mode=pl.Buffered(1))


def _tile_index(i, lead):
    return jnp.maximum(i - lead, 0)


def _row_spec(tm, width, lead):
    return pl.BlockSpec((tm, width), lambda i: (_tile_index(i, lead), 0))


def _mod_specs(mod, layer, first, tiles_per_seq, lead):
    if mod.ndim == 4:
        return [pl.BlockSpec((None, None, 1, D_MODEL),
                             lambda i, k=k: (layer, _tile_index(i, lead) // tiles_per_seq, 0, k))
                for k in range(first, first + 3)]
    b = mod.shape[1]
    return [pl.BlockSpec((None, b, D_MODEL), lambda i, k=k: (layer, 0, k)) for k in range(first, first + 3)]


def _norm_spec(layer, which):
    return pl.BlockSpec((None, 1, D_MODEL), lambda i: (4 * layer + which, 0, 0))


def _weight_plumbing(weights, index):
    if weights[0].dtype == BF16:
        return 0, [_resident(w.shape, lambda i: (0, 0)) for w in weights], [], [], []
    in_specs, out_specs, out_shapes, scratch = [], [], [], []
    for w in weights:
        _, r, n = w.shape
        chunk = r // CAST_STEPS
        in_specs.append(pl.BlockSpec((None, chunk, n), lambda i: (index, jnp.minimum(i, CAST_STEPS - 1), 0)))
        out_specs.append(pl.BlockSpec((chunk, n), lambda i: (jnp.minimum(i, CAST_STEPS - 1), 0)))
        out_shapes.append(jax.ShapeDtypeStruct((r, n), BF16))
        scratch.append(pltpu.VMEM((r, n), BF16))
    return CAST_STEPS, in_specs, out_specs, out_shapes, scratch


_PARAMS = pltpu.CompilerParams(dimension_semantics=("arbitrary",), vmem_limit_bytes=VMEM_LIMIT_BYTES)


def _ada_kernel(c_ref, w_ref, b_ref, os_ref, op_ref):
    c = c_ref[...]
    s = c * jax.nn.sigmoid(c)
    m = _mm(s, w_ref[...].astype(BF16)) + b_ref[...]
    n_s = os_ref.shape[0]
    os_ref[...] = m[:n_s]
    op_ref[...] = m[n_s:]


def _ada_call(c_all, ada_w, ada_b, n_s, n_p):
    tn = 1536
    n_all = c_all.shape[0]
    return pl.pallas_call(
        _ada_kernel,
        grid=(DEPTH, 6 * D_MODEL // tn),
        in_specs=[
            pl.BlockSpec((n_all, D_MODEL), lambda l, n: (0, 0)),
            pl.BlockSpec((None, D_MODEL, tn), lambda l, n: (l, 0, n)),
            pl.BlockSpec((None, 1, tn), lambda l, n: (l, 0, n)),
        ],
        out_specs=[
            pl.BlockSpec((None, n_s, tn), lambda l, n: (l, 0, n)),
            pl.BlockSpec((None, n_p, tn), lambda l, n: (l, 0, n)),
        ],
        out_shape=[
            jax.ShapeDtypeStruct((DEPTH, n_s, 6 * D_MODEL), F32),
            jax.ShapeDtypeStruct((DEPTH, n_p, 6 * D_MODEL), F32),
        ],
        compiler_params=pltpu.CompilerParams(
            dimension_semantics=("arbitrary", "arbitrary"), vmem_limit_bytes=VMEM_LIMIT_BYTES),
        name="ada_mod",
    )(c_all, ada_w, ada_b.reshape(DEPTH, 1, 6 * D_MODEL))


def _mlp_kernel(x_ref, sh_ref, sc_ref, g_ref, n2_ref, n3_ref, wup_in, wdn_in, o_ref, *rest,
                groups, subtiles, lead):
    def body(wup_ref, wdn_ref):
        rows = x_ref.shape[0] // subtiles
        for s in range(subtiles):
            x = x_ref[s * rows:(s + 1) * rows, :]
            hf = _norm_modulate(x, n2_ref[...], sc_ref[...], sh_ref[...], groups // subtiles)
            hid = jnp.square(jnp.maximum(_mm(hf, wup_ref[...]), 0.0))
            f = _mm(hid, wdn_ref[...])
            o_ref[s * rows:(s + 1) * rows, :] = _norm_gated_residual(x, g_ref[...], f, n3_ref[...],
                                                                     groups // subtiles)

    if not lead:
        body(wup_in, wdn_in)
        return
    wup_o, wdn_o, wup_s, wdn_s = rest
    step = pl.program_id(0)
    pl.when(step < lead)(lambda: _cast_weight_chunks(step, (wup_in, wdn_in), (wup_o, wdn_o), (wup_s, wdn_s)))
    pl.when(step >= lead)(lambda: body(wup_s, wdn_s))


def _mlp_call(x, mod, norm_g, weights, layer, tm, groups, tiles_per_seq, name, subtiles=2):
    rows = x.shape[0]
    lead, w_specs, w_out_specs, w_out_shapes, w_scratch = _weight_plumbing(weights, layer)
    row_spec = _row_spec(tm, D_MODEL, lead)
    outs = pl.pallas_call(
        functools.partial(_mlp_kernel, groups=groups, subtiles=subtiles, lead=lead),
        grid=(lead + rows // tm,),
        in_specs=[row_spec, *_mod_specs(mod, layer, 3, tiles_per_seq, lead), _norm_spec(layer, 2),
                  _norm_spec(layer, 3), *w_specs],
        out_specs=[row_spec, *w_out_specs],
        out_shape=[jax.ShapeDtypeStruct((rows, D_MODEL), F32), *w_out_shapes],
        scratch_shapes=w_scratch,
        compiler_params=_PARAMS,
        name=name,
    )(x, mod, mod, mod, norm_g, norm_g, *weights)
    return outs[0], tuple(outs[1:])


def _lru_kernel(x_ref, sh_ref, sc_ref, g_ref, n0_ref, n1_ref, cw_ref, cb_ref, wa_ref, ba_ref,
                wx_ref, bx_ref, lam_ref, conv0_ref, h0_ref, win_in, wout_in,
                o_ref, convo_ref, ho_ref, *rest, steps, batch, subtiles, lead):
    if lead:
        win_o, wout_o, tail_ref, hc_ref, hs_ref, win_s, wout_s = rest
    else:
        tail_ref, hc_ref, hs_ref = rest
    step = pl.program_id(0)

    @pl.when(step == 0)
    def _():
        tail_ref[...] = conv0_ref[...]
        hc_ref[...] = h0_ref[...]

    def body(win_ref, wout_ref):
        nlam = -lam_ref[...]
        neg_c_softplus = -LRU_C * (jnp.maximum(nlam, 0.0) + jnp.log1p(jnp.exp(-jnp.abs(nlam))))
        cw = cw_ref[...]
        tail = tail_ref[...]
        h = hc_ref[...]
        sub_steps = steps // subtiles
        rows = sub_steps * batch
        for s in range(subtiles):
            r0 = s * rows
            x = x_ref[r0:r0 + rows, :]
            hin = _norm_modulate(x, n0_ref[...], sc_ref[...], sh_ref[...], sub_steps)
            z = _mm(hin, win_ref[...])
            gate_br = z[:, :D_RNN]
            x_br = z[:, D_RNN:]

            xp = jnp.concatenate([tail, x_br], axis=0)
            xc = cb_ref[...]
            for k in range(CONV_W):
                xc = xc + xp[k * batch:k * batch + rows] * cw[k:k + 1]
            tail = xp[rows:]

            xcb = xc.astype(BF16)
            r = jax.nn.sigmoid(_block_diag_mm(xcb, wa_ref, LRU_BLOCK) + ba_ref[...])
            ig = jax.nn.sigmoid(_block_diag_mm(xcb, wx_ref, LRU_BLOCK) + bx_ref[...])
            log_a = r * neg_c_softplus
            a = jnp.exp(log_a)
            q = jnp.tanh(log_a) * (-1.0 - a * a)
            mult = jnp.where(q > 0.0, q * lax.rsqrt(q), 0.0)
            bt = mult * (ig * xc)

            for t in range(sub_steps):
                h = a[t * batch:(t + 1) * batch] * h + bt[t * batch:(t + 1) * batch]
                hs_ref[r0 + t * batch:r0 + (t + 1) * batch, :] = h

            y = hs_ref[r0:r0 + rows, :] * _gelu(gate_br)
            mix = _mm(y, wout_ref[...])
            o_ref[r0:r0 + rows, :] = _norm_gated_residual(x, g_ref[...], mix, n1_ref[...], sub_steps)

        tail_ref[...] = tail
        convo_ref[...] = tail
        hc_ref[...] = h
        ho_ref[...] = h

    if not lead:
        body(win_in, wout_in)
        return
    pl.when(step < lead)(lambda: _cast_weight_chunks(step, (win_in, wout_in), (win_o, wout_o), (win_s, wout_s)))
    pl.when(step >= lead)(lambda: body(win_s, wout_s))


def _lru_call(x, mod, norm_g, p, weights, conv0, h0, layer, steps, batch, name, subtiles=2):
    j = layer // 2
    tm = steps * batch
    rows = x.shape[0]
    tail_rows = (CONV_W - 1) * batch
    lead, w_specs, w_out_specs, w_out_shapes, w_scratch = _weight_plumbing(weights, j)
    row_spec = _row_spec(tm, D_MODEL, lead)
    vec = lambda: pl.BlockSpec((None, 1, D_RNN), lambda i: (j, 0, 0))
    band = lambda: _resident((None, D_RNN // MXU_N, BAND_K, MXU_N), lambda i: (j, 0, 0, 0))
    conv_spec = pl.BlockSpec((tail_rows, D_RNN), lambda i: (0, 0))
    h_spec = pl.BlockSpec((batch, D_RNN), lambda i: (0, 0))
    outs = pl.pallas_call(
        functools.partial(_lru_kernel, steps=steps, batch=batch, subtiles=subtiles, lead=lead),
        grid=(lead + rows // tm,),
        in_specs=[row_spec, *_mod_specs(mod, layer, 0, 1, lead), _norm_spec(layer, 0), _norm_spec(layer, 1),
                  pl.BlockSpec((None, CONV_W, D_RNN), lambda i: (j, 0, 0)), vec(),
                  band(), vec(), band(), vec(), vec(), conv_spec, h_spec, *w_specs],
        out_specs=[row_spec, conv_spec, h_spec, *w_out_specs],
        out_shape=[jax.ShapeDtypeStruct((rows, D_MODEL), F32),
                   jax.ShapeDtypeStruct((tail_rows, D_RNN), F32),
                   jax.ShapeDtypeStruct((batch, D_RNN), F32), *w_out_shapes],
        scratch_shapes=[pltpu.VMEM((tail_rows, D_RNN), F32), pltpu.VMEM((batch, D_RNN), F32),
                        pltpu.VMEM((tm, D_RNN), F32), *w_scratch],
        compiler_params=_PARAMS,
        name=name,
    )(x, mod, mod, mod, norm_g, norm_g, p["conv_w"], p["conv_b"], p["wa"], p["ba"],
      p["wx"], p["bx"], p["lam"], conv0, h0, *weights)
    return outs[0], outs[1], outs[2], tuple(outs[3:])


def _sgu_front(x, sh_ref, sc_ref, n0_ref, win_ref, bin_ref, lng_ref, lnb_ref, groups):
    hin = _norm_modulate(x, n0_ref[...], sc_ref[...], sh_ref[...], groups)
    z = _gelu(_mm(hin, win_ref[...]) + bin_ref[...])
    u = z[:, :D_SG]
    v = z[:, D_SG:]
    vc = v - jnp.mean(v, axis=-1, keepdims=True)
    var = jnp.mean(vc * vc, axis=-1, keepdims=True)
    vn = vc * lax.rsqrt(var + EPS) * lng_ref[...] + lnb_ref[...]
    return u, vn


def _sgu_prompt_kernel(x_ref, sh_ref, sc_ref, g_ref, n0_ref, n1_ref, bin_ref, lng_ref, lnb_ref,
                       ws_ref, bsx_ref, win_in, wout_in, o_ref, win_o, wout_o, wsp_ref, win_s, wout_s,
                       *, chunks, subtiles, lead):
    step = pl.program_id(0)
    n_groups = D_SG // SG_GROUP

    @pl.when(step == 0)
    def _():
        t_idx = lax.broadcasted_iota(jnp.int32, (CHUNK, CHUNK), 0)
        s_idx = lax.broadcasted_iota(jnp.int32, (CHUNK, CHUNK), 1)
        for g in range(n_groups):
            wsp_ref[g] = jnp.where(s_idx <= t_idx, ws_ref[g], 0.0).astype(BF16)

    def body():
        lane = lax.broadcasted_iota(jnp.int32, (1, MXU_N), 1)
        sub_chunks = chunks // subtiles
        rows = sub_chunks * CHUNK
        for s in range(subtiles):
            x = x_ref[s * rows:(s + 1) * rows, :]
            u, vn = _sgu_front(x, sh_ref, sc_ref, n0_ref, win_s, bin_ref, lng_ref, lnb_ref, 1)
            mixed_rows = []
            for c in range(sub_chunks):
                tiles = []
                for c0 in range(0, D_SG, MXU_N):
                    v_tile = vn[c * CHUNK:(c + 1) * CHUNK, c0:c0 + MXU_N]
                    acc = None
                    for g in range(c0 // SG_GROUP, (c0 + MXU_N - 1) // SG_GROUP + 1):
                        in_group = (lane >= g * SG_GROUP - c0) & (lane < (g + 1) * SG_GROUP - c0)
                        part = jnp.dot(wsp_ref[g], jnp.where(in_group, v_tile, 0.0).astype(BF16),
                                       preferred_element_type=F32)
                        acc = part if acc is None else acc + part
                    tiles.append(acc)
                mixed_rows.append(jnp.concatenate(tiles, axis=1) + bsx_ref[...])
            mixed = jnp.concatenate(mixed_rows, axis=0)
            mix = _mm(u * mixed, wout_s[...])
            o_ref[s * rows:(s + 1) * rows, :] = _norm_gated_residual(x, g_ref[...], mix, n1_ref[...], 1)

    pl.when(step < lead)(lambda: _cast_weight_chunks(step, (win_in, wout_in), (win_o, wout_o), (win_s, wout_s)))
    pl.when(step >= lead)(body)


def _sgu_sample_kernel(x_ref, sh_ref, sc_ref, g_ref, n0_ref, n1_ref, bin_ref, lng_ref, lnb_ref,
                       coef_ref, bias_ref, win_ref, wout_ref, o_ref, v_ref, *, steps, batch):
    x = x_ref[...]
    u, vn = _sgu_front(x, sh_ref, sc_ref, n0_ref, win_ref, bin_ref, lng_ref, lnb_ref, steps)
    v_ref[...] = vn
    slabs = []
    for t in range(steps):
        acc = None
        for s in range(t + 1):
            term = coef_ref[t * steps + s:t * steps + s + 1, :] * vn[s * batch:(s + 1) * batch]
            acc = term if acc is None else acc + term
        slabs.append(acc + bias_ref[t:t + 1, :])
    mixed = jnp.concatenate(slabs, axis=0)
    mix = _mm(u * mixed, wout_ref[...])
    o_ref[...] = _norm_gated_residual(x, g_ref[...], mix, n1_ref[...], steps)


def _sgu_vec_specs(layer):
    j = layer // 2
    return [pl.BlockSpec((None, 1, 2 * D_SG), lambda i: (j, 0, 0)),
            pl.BlockSpec((None, 1, D_SG), lambda i: (j, 0, 0)),
            pl.BlockSpec((None, 1, D_SG), lambda i: (j, 0, 0))]


def _sgu_prompt_call(x, mod, norm_g, p, weights, layer, tm, seq_len, name, subtiles=2):
    j = layer // 2
    rows = x.shape[0]
    n_groups = D_SG // SG_GROUP
    lead, w_specs, w_out_specs, w_out_shapes, w_scratch = _weight_plumbing(weights, j)
    row_spec = _row_spec(tm, D_MODEL, lead)
    outs = pl.pallas_call(
        functools.partial(_sgu_prompt_kernel, chunks=tm // CHUNK, subtiles=subtiles, lead=lead),
        grid=(lead + rows // tm,),
        in_specs=[row_spec, *_mod_specs(mod, layer, 0, seq_len // tm, lead), _norm_spec(layer, 0),
                  _norm_spec(layer, 1), *_sgu_vec_specs(layer),
                  pl.BlockSpec((None, n_groups, CHUNK, CHUNK), lambda i: (j, 0, 0, 0)),
                  pl.BlockSpec((None, CHUNK, D_SG), lambda i: (j, 0, 0)), *w_specs],
        out_specs=[row_spec, *w_out_specs],
        out_shape=[jax.ShapeDtypeStruct((rows, D_MODEL), F32), *w_out_shapes],
        scratch_shapes=[pltpu.VMEM((n_groups, CHUNK, CHUNK), BF16), *w_scratch],
        compiler_params=_PARAMS,
        name=name,
    )(x, mod, mod, mod, norm_g, norm_g, p["b_in"], p["ln_g"], p["ln_b"], p["ws"], p["bsx"], *weights)
    return outs[0], tuple(outs[1:])


def _sgu_sample_call(x, mod, norm_g, p, weights, layer, steps, batch, name):
    j = layer // 2
    rows = x.shape[0]
    _, w_specs, _, _, _ = _weight_plumbing(weights, j)
    row_spec = pl.BlockSpec((rows, D_MODEL), lambda i: (0, 0))
    v_spec = pl.BlockSpec((rows, D_SG), lambda i: (0, 0))
    return pl.pallas_call(
        functools.partial(_sgu_sample_kernel, steps=steps, batch=batch),
        grid=(1,),
        in_specs=[row_spec, *_mod_specs(mod, layer, 0, 1, 0), _norm_spec(layer, 0), _norm_spec(layer, 1),
                  *_sgu_vec_specs(layer),
                  pl.BlockSpec((None, steps * steps, D_SG), lambda i: (j, 0, 0)),
                  pl.BlockSpec((None, steps, D_SG), lambda i: (j, 0, 0)), *w_specs],
        out_specs=[row_spec, v_spec],
        out_shape=[jax.ShapeDtypeStruct((rows, D_MODEL), F32), jax.ShapeDtypeStruct((rows, D_SG), F32)],
        compiler_params=_PARAMS,
        name=name,
    )(x, mod, mod, mod, norm_g, norm_g, p["b_in"], p["ln_g"], p["ln_b"], p["coef"], p["bias"], *weights)


def _band_pack(w):
    n_l, n_h, b, _ = w.shape
    eye = jnp.eye(n_h, dtype=w.dtype)
    dense = (w[:, :, :, None, :] * eye[None, :, None, :, None]).reshape(n_l, n_h * b, n_h * b)
    tiles = [dense[:, k0:k0 + BAND_K, c0:c0 + MXU_N] for c0, k0 in _band_starts(n_h * b, b)]
    return jnp.stack(tiles, axis=1).astype(BF16)


def _to_time_major(x):
    n, t, d = x.shape
    return x.transpose(1, 0, 2).reshape(t * n, d)


def _from_time_major(x, n):
    rows, d = x.shape
    return x.reshape(rows // n, n, d).transpose(1, 0, 2)


def kernel(x_prompt, x_sample, c_prompt, c_sample, state_lru_h, state_lru_conv, ada_w, ada_b, norm_g, lru_w_in, lru_conv_w, lru_conv_b, lru_wa, lru_ba, lru_wx, lru_bx, lru_lambda, lru_w_out, sg_w_in, sg_b_in, sg_ln_g, sg_ln_b, sg_ws, sg_bs, sg_w_out, mlp_w_up, mlp_w_down):
    n_p, t_p, _ = x_prompt.shape
    n_s, t_s, _ = x_sample.shape
    n_lru = lru_w_in.shape[0]
    n_sg = sg_w_in.shape[0]

    mod_s, mod_p = _ada_call(jnp.concatenate([c_sample, c_prompt], axis=0), ada_w, ada_b, n_s, n_p)
    mod_p_seq = mod_p.reshape(DEPTH, n_p, 1, 6 * D_MODEL)
    norms = norm_g.reshape(DEPTH * 4, 1, D_MODEL)

    lru = dict(
        conv_w=lru_conv_w, conv_b=lru_conv_b.reshape(n_lru, 1, D_RNN),
        wa=_band_pack(lru_wa), ba=lru_ba.reshape(n_lru, 1, D_RNN),
        wx=_band_pack(lru_wx), bx=lru_bx.reshape(n_lru, 1, D_RNN),
        lam=lru_lambda.reshape(n_lru, 1, D_RNN))
    sgu = dict(
        b_in=sg_b_in.reshape(n_sg, 1, 2 * D_SG),
        ln_g=sg_ln_g.reshape(n_sg, 1, D_SG), ln_b=sg_ln_b.reshape(n_sg, 1, D_SG),
        ws=sg_ws,
        bsx=jnp.repeat(sg_bs.transpose(0, 2, 1), SG_GROUP, axis=2),
        coef=jnp.repeat(sg_ws[:, :, :t_s, :t_s].transpose(0, 2, 3, 1), SG_GROUP, axis=3).reshape(n_sg, t_s * t_s, D_SG),
        bias=jnp.repeat(sg_bs[:, :, :t_s].transpose(0, 2, 1), SG_GROUP, axis=2))

    tm = 512
    steps_p = tm // n_p
    xp = x_prompt.reshape(n_p * t_p, D_MODEL)
    xs = _to_time_major(x_sample)
    zero_conv = jnp.zeros(((CONV_W - 1) * n_p, D_RNN), F32)
    zero_h = jnp.zeros((n_p, D_RNN), F32)
    h_p, conv_p, h_s, conv_s, v_s = [], [], [], [], []
    for layer in range(DEPTH):
        j = layer // 2
        if layer % 2 == 0:
            xp = _to_time_major(xp.reshape(n_p, t_p, D_MODEL))
            xp, cp, hp, w_mix = _lru_call(xp, mod_p, norms, lru, (lru_w_in, lru_w_out), zero_conv, zero_h,
                                          layer, steps_p, n_p, f"lru_prompt_{layer}")
            xp, w_mlp = _mlp_call(xp, mod_p, norms, (mlp_w_up, mlp_w_down), layer, tm, steps_p, 1,
                                  f"mlp_prompt_{layer}")
            xp = _from_time_major(xp, n_p).reshape(n_p * t_p, D_MODEL)
            xs, cs, hs, _ = _lru_call(xs, mod_s, norms, lru, w_mix, _to_time_major(state_lru_conv[j]),
                                      state_lru_h[j], layer, t_s, n_s, f"lru_sample_{layer}")
            h_p.append(hp)
            conv_p.append(_from_time_major(cp, n_p))
            h_s.append(hs)
            conv_s.append(_from_time_major(cs, n_s))
        else:
            xp, w_mix = _sgu_prompt_call(xp, mod_p_seq, norms, sgu, (sg_w_in, sg_w_out), layer, tm, t_p,
                                         f"sgu_prompt_{layer}")
            xp, w_mlp = _mlp_call(xp, mod_p_seq, norms, (mlp_w_up, mlp_w_down), layer, tm, 1, t_p // tm,
                                  f"mlp_prompt_{layer}")
            xs, v = _sgu_sample_call(xs, mod_s, norms, sgu, w_mix, layer, t_s, n_s, f"sgu_sample_{layer}")
            v_s.append(_from_time_major(v, n_s))
        xs, _ = _mlp_call(xs, mod_s, norms, w_mlp, layer, t_s * n_s, t_s, 1, f"mlp_sample_{layer}")

    y_prompt = xp.reshape(n_p, t_p, D_MODEL)
    y_sample = _from_time_major(xs, n_s)
    return (y_prompt, y_sample, jnp.stack(h_p), jnp.stack(conv_p), jnp.stack(h_s), jnp.stack(conv_s),
            jnp.stack(v_s))
```

```python
import functools

import jax
import jax.numpy as jnp
from jax import lax
from jax.experimental import pallas as pl
from jax.experimental.pallas import tpu as pltpu

D_MODEL = 1024
D_RNN = 1536
D_SG = 1536
D_FF = 4096
DEPTH = 4
CONV_W = 4
LRU_BLOCK = 96
SG_GROUP = 96
CHUNK = 128
LRU_C = 8.0
EPS = 1e-6

MXU_N = 256
LANES = 128
BAND_K = 512
VMEM_LIMIT_BYTES = 56 * 1024 * 1024
CAST_STEPS = 8

F32 = jnp.float32
BF16 = jnp.bfloat16


def _row_rsqrt_ms(x):
    return lax.rsqrt(jnp.mean(x * x, axis=-1, keepdims=True) + EPS)


def _slabs(fn, x, vecs, seq_major):
    b = vecs[0].shape[0]
    if b == 1:
        return fn(x, *vecs)
    rows, d = x.shape
    if seq_major:
        return fn(x.reshape(b, rows // b, d), *[v[:, None, :] for v in vecs]).reshape(rows, d)
    return fn(x.reshape(rows // b, b, d), *[v[None] for v in vecs]).reshape(rows, d)


def _norm_modulate(x, g, sc, sh, seq_major=False):
    return _slabs(lambda xs, scale, shift: xs * scale + shift, x * _row_rsqrt_ms(x), (g * (1.0 + sc), sh), seq_major)


def _norm_gated_residual(x, gate, y, g, seq_major=False):
    yn = _slabs(lambda ys, scale: ys * scale, y * _row_rsqrt_ms(y), (gate * g,), seq_major)
    return x + yn


def _load_rows(x_ref, s, n, time_major):
    if len(x_ref.shape) == 2:
        rows = x_ref.shape[0] // n
        return x_ref[s * rows:(s + 1) * rows, :]
    b, t, d = x_ref.shape
    ts = t // n
    x3 = x_ref[:, s * ts:(s + 1) * ts, :]
    if time_major:
        x3 = jnp.swapaxes(x3, 0, 1)
    return x3.reshape(b * ts, d)


def _store_rows(o_ref, s, n, y, batch, time_major):
    rows, d = y.shape
    if len(o_ref.shape) == 2:
        if batch > 1 and not time_major:
            y = jnp.swapaxes(y.reshape(batch, rows // batch, d), 0, 1).reshape(rows, d)
        o_ref[s * rows:(s + 1) * rows, :] = y
        return
    b, t, _ = o_ref.shape
    ts = t // n
    y3 = jnp.swapaxes(y.reshape(ts, b, d), 0, 1) if time_major else y.reshape(b, ts, d)
    o_ref[:, s * ts:(s + 1) * ts, :] = y3


def _gelu(x):
    k = 0.7978845608028654
    t = jnp.tanh(x * (k + (k * 0.044715) * (x * x)))
    return x * (0.5 + 0.5 * t)


def _mm(a, w):
    return jnp.dot(a.astype(BF16), w, preferred_element_type=F32)


def _band_starts(width, block):
    out = []
    for c0 in range(0, width, MXU_N):
        lo = (c0 // block) * block
        hi = ((c0 + MXU_N - 1) // block + 1) * block
        k0 = min((lo // LANES) * LANES, width - BAND_K)
        assert k0 <= lo and hi <= k0 + BAND_K
        out.append((c0, k0))
    return out


def _block_diag_mm(xb, w_ref, block):
    tiles = [jnp.dot(xb[:, k0:k0 + BAND_K], w_ref[t], preferred_element_type=F32)
             for t, (_, k0) in enumerate(_band_starts(xb.shape[1], block))]
    return jnp.concatenate(tiles, axis=1)


def _cast_weight_chunks(step, srcs, outs, dsts):
    for src, out, dst in zip(srcs, outs, dsts):
        rows = src.shape[0]
        wb = src[...].astype(BF16)
        out[...] = wb
        dst[pl.ds(pl.multiple_of(step * rows, rows), rows), :] = wb


def _resident(shape, index_map):
    return pl.BlockSpec(shape, index_map, pipeline_mode=pl.Buffered(1))


def _tile_index(i, lead):
    return jnp.maximum(i - lead, 0)


def _row_spec(tm, width, lead):
    return pl.BlockSpec((tm, width), lambda i: (_tile_index(i, lead), 0))


def _mod_specs(mod, layer, first, tiles_per_seq, lead):
    if mod.ndim == 4:
        return [pl.BlockSpec((None, None, 1, D_MODEL),
                             lambda i, k=k: (layer, _tile_index(i, lead) // tiles_per_seq, 0, k))
                for k in range(first, first + 3)]
    b = mod.shape[1]
    return [pl.BlockSpec((None, b, D_MODEL), lambda i, k=k: (layer, 0, k)) for k in range(first, first + 3)]


def _norm_spec(layer, which):
    return pl.BlockSpec((None, 1, D_MODEL), lambda i: (4 * layer + which, 0, 0))


def _weight_plumbing(weights, index):
    if weights[0].dtype == BF16:
        return 0, [_resident(w.shape, lambda i: (0, 0)) for w in weights], [], [], []
    in_specs, out_specs, out_shapes, scratch = [], [], [], []
    for w in weights:
        _, r, n = w.shape
        chunk = r // CAST_STEPS
        in_specs.append(pl.BlockSpec((None, chunk, n), lambda i: (index, jnp.minimum(i, CAST_STEPS - 1), 0)))
        out_specs.append(pl.BlockSpec((chunk, n), lambda i: (jnp.minimum(i, CAST_STEPS - 1), 0)))
        out_shapes.append(jax.ShapeDtypeStruct((r, n), BF16))
        scratch.append(pltpu.VMEM((r, n), BF16))
    return CAST_STEPS, in_specs, out_specs, out_shapes, scratch


_PARAMS = pltpu.CompilerParams(dimension_semantics=("arbitrary",), vmem_limit_bytes=VMEM_LIMIT_BYTES)


def _ada_kernel(c_ref, w_ref, b_ref, os_ref, op_ref):
    c = c_ref[...]
    s = c * jax.nn.sigmoid(c)
    m = _mm(s, w_ref[...].astype(BF16)) + b_ref[...]
    n_s = os_ref.shape[0]
    os_ref[...] = m[:n_s]
    op_ref[...] = m[n_s:]


def _ada_call(c_all, ada_w, ada_b, n_s, n_p):
    tn = 1536
    n_all = c_all.shape[0]
    return pl.pallas_call(
        _ada_kernel,
        grid=(DEPTH, 6 * D_MODEL // tn),
        in_specs=[
            pl.BlockSpec((n_all, D_MODEL), lambda l, n: (0, 0)),
            pl.BlockSpec((None, D_MODEL, tn), lambda l, n: (l, 0, n)),
            pl.BlockSpec((None, 1, tn), lambda l, n: (l, 0, n)),
        ],
        out_specs=[
            pl.BlockSpec((None, n_s, tn), lambda l, n: (l, 0, n)),
            pl.BlockSpec((None, n_p, tn), lambda l, n: (l, 0, n)),
        ],
        out_shape=[
            jax.ShapeDtypeStruct((DEPTH, n_s, 6 * D_MODEL), F32),
            jax.ShapeDtypeStruct((DEPTH, n_p, 6 * D_MODEL), F32),
        ],
        compiler_params=pltpu.CompilerParams(
            dimension_semantics=("arbitrary", "arbitrary"), vmem_limit_bytes=VMEM_LIMIT_BYTES),
        name="ada_mod",
    )(c_all, ada_w, ada_b.reshape(DEPTH, 1, 6 * D_MODEL))


def _mlp_kernel(x_ref, sh_ref, sc_ref, g_ref, n2_ref, n3_ref, wup_in, wdn_in, o_ref, *rest,
                batch, time_major, subtiles, lead):
    def body(wup_ref, wdn_ref):
        for s in range(subtiles):
            x = _load_rows(x_ref, s, subtiles, time_major)
            hf = _norm_modulate(x, n2_ref[...], sc_ref[...], sh_ref[...], not time_major)
            hid = jnp.square(jnp.maximum(_mm(hf, wup_ref[...]), 0.0))
            f = _mm(hid, wdn_ref[...])
            y = _norm_gated_residual(x, g_ref[...], f, n3_ref[...], not time_major)
            _store_rows(o_ref, s, subtiles, y, batch, time_major)

    if not lead:
        body(wup_in, wdn_in)
        return
    wup_o, wdn_o, wup_s, wdn_s = rest
    step = pl.program_id(0)
    pl.when(step < lead)(lambda: _cast_weight_chunks(step, (wup_in, wdn_in), (wup_o, wdn_o), (wup_s, wdn_s)))
    pl.when(step >= lead)(lambda: body(wup_s, wdn_s))


def _act_spec(shape, tm, lead):
    if len(shape) == 2:
        return _row_spec(tm, shape[1], lead)
    n, _, d = shape
    return pl.BlockSpec((n, tm // n, d), lambda i: (0, _tile_index(i, lead), 0))


def _mlp_call(x, mod, norm_g, weights, layer, tm, out_shape, batch, time_major, tiles_per_seq, name, subtiles=2):
    n_tiles = x.size // (tm * D_MODEL)
    lead, w_specs, w_out_specs, w_out_shapes, w_scratch = _weight_plumbing(weights, layer)
    outs = pl.pallas_call(
        functools.partial(_mlp_kernel, batch=batch, time_major=time_major, subtiles=subtiles, lead=lead),
        grid=(lead + n_tiles,),
        in_specs=[_act_spec(x.shape, tm, lead), *_mod_specs(mod, layer, 3, tiles_per_seq, lead),
                  _norm_spec(layer, 2), _norm_spec(layer, 3), *w_specs],
        out_specs=[_act_spec(out_shape, tm, lead), *w_out_specs],
        out_shape=[jax.ShapeDtypeStruct(out_shape, F32), *w_out_shapes],
        scratch_shapes=w_scratch,
        compiler_params=_PARAMS,
        name=name,
    )(x, mod, mod, mod, norm_g, norm_g, *weights)
    return outs[0], tuple(outs[1:])


def _lru_kernel(x_ref, sh_ref, sc_ref, g_ref, n0_ref, n1_ref, cw_ref, cb_ref, wa_ref, ba_ref,
                wx_ref, bx_ref, lam_ref, conv0_ref, h0_ref, win_in, wout_in,
                o_ref, convo_ref, ho_ref, *rest, steps, batch, subtiles, lead):
    if lead:
        win_o, wout_o, tail_ref, hc_ref, hs_ref, win_s, wout_s = rest
    else:
        tail_ref, hc_ref, hs_ref = rest
    step = pl.program_id(0)

    @pl.when(step == 0)
    def _():
        tail_ref[...] = conv0_ref[...]
        hc_ref[...] = h0_ref[...]

    def body(win_ref, wout_ref):
        nlam = -lam_ref[...]
        neg_c_softplus = -LRU_C * (jnp.maximum(nlam, 0.0) + jnp.log1p(jnp.exp(-jnp.abs(nlam))))
        cw = cw_ref[...]
        tail = tail_ref[...]
        h = hc_ref[...]
        sub_steps = steps // subtiles
        rows = sub_steps * batch
        for s in range(subtiles):
            r0 = s * rows
            x = _load_rows(x_ref, s, subtiles, True)
            hin = _norm_modulate(x, n0_ref[...], sc_ref[...], sh_ref[...])
            z = _mm(hin, win_ref[...])
            gate_br = z[:, :D_RNN]
            x_br = z[:, D_RNN:]

            xp = jnp.concatenate([tail, x_br], axis=0)
            xc = cb_ref[...]
            for k in range(CONV_W):
                xc = xc + xp[k * batch:k * batch + rows] * cw[k:k + 1]
            tail = xp[rows:]

            xcb = xc.astype(BF16)
            r = jax.nn.sigmoid(_block_diag_mm(xcb, wa_ref, LRU_BLOCK) + ba_ref[...])
            ig = jax.nn.sigmoid(_block_diag_mm(xcb, wx_ref, LRU_BLOCK) + bx_ref[...])
            log_a = r * neg_c_softplus
            a = jnp.exp(log_a)
            q = jnp.tanh(log_a) * (-1.0 - a * a)
            mult = jnp.where(q > 0.0, q * lax.rsqrt(q), 0.0)
            bt = mult * (ig * xc)

            for t in range(sub_steps):
                h = a[t * batch:(t + 1) * batch] * h + bt[t * batch:(t + 1) * batch]
                hs_ref[r0 + t * batch:r0 + (t + 1) * batch, :] = h

            y = hs_ref[r0:r0 + rows, :] * _gelu(gate_br)
            mix = _mm(y, wout_ref[...])
            o_ref[r0:r0 + rows, :] = _norm_gated_residual(x, g_ref[...], mix, n1_ref[...])

        tail_ref[...] = tail
        convo_ref[...] = tail
        hc_ref[...] = h
        ho_ref[...] = h

    if not lead:
        body(win_in, wout_in)
        return
    pl.when(step < lead)(lambda: _cast_weight_chunks(step, (win_in, wout_in), (win_o, wout_o), (win_s, wout_s)))
    pl.when(step >= lead)(lambda: body(win_s, wout_s))


def _lru_call(x, mod, norm_g, p, weights, conv0, h0, layer, steps, batch, name, subtiles=2):
    j = layer // 2
    tm = steps * batch
    rows = x.size // D_MODEL
    tail_rows = (CONV_W - 1) * batch
    lead, w_specs, w_out_specs, w_out_shapes, w_scratch = _weight_plumbing(weights, j)
    row_spec = _row_spec(tm, D_MODEL, lead)
    vec = lambda: pl.BlockSpec((None, 1, D_RNN), lambda i: (j, 0, 0))
    band = lambda: _resident((None, D_RNN // MXU_N, BAND_K, MXU_N), lambda i: (j, 0, 0, 0))
    conv_spec = pl.BlockSpec((tail_rows, D_RNN), lambda i: (0, 0))
    h_spec = pl.BlockSpec((batch, D_RNN), lambda i: (0, 0))
    outs = pl.pallas_call(
        functools.partial(_lru_kernel, steps=steps, batch=batch, subtiles=subtiles, lead=lead),
        grid=(lead + rows // tm,),
        in_specs=[_act_spec(x.shape, tm, lead), *_mod_specs(mod, layer, 0, 1, lead), _norm_spec(layer, 0),
                  _norm_spec(layer, 1),
                  pl.BlockSpec((None, CONV_W, D_RNN), lambda i: (j, 0, 0)), vec(),
                  band(), vec(), band(), vec(), vec(), conv_spec, h_spec, *w_specs],
        out_specs=[row_spec, conv_spec, h_spec, *w_out_specs],
        out_shape=[jax.ShapeDtypeStruct((rows, D_MODEL), F32),
                   jax.ShapeDtypeStruct((tail_rows, D_RNN), F32),
                   jax.ShapeDtypeStruct((batch, D_RNN), F32), *w_out_shapes],
        scratch_shapes=[pltpu.VMEM((tail_rows, D_RNN), F32), pltpu.VMEM((batch, D_RNN), F32),
                        pltpu.VMEM((tm, D_RNN), F32), *w_scratch],
        compiler_params=_PARAMS,
        name=name,
    )(x, mod, mod, mod, norm_g, norm_g, p["conv_w"], p["conv_b"], p["wa"], p["ba"],
      p["wx"], p["bx"], p["lam"], conv0, h0, *weights)
    return outs[0], outs[1], outs[2], tuple(outs[3:])


def _sgu_front(x, sh_ref, sc_ref, n0_ref, win_ref, bin_ref, lng_ref, lnb_ref):
    hin = _norm_modulate(x, n0_ref[...], sc_ref[...], sh_ref[...])
    z = _gelu(_mm(hin, win_ref[...]) + bin_ref[...])
    u = z[:, :D_SG]
    v = z[:, D_SG:]
    vc = v - jnp.mean(v, axis=-1, keepdims=True)
    var = jnp.mean(vc * vc, axis=-1, keepdims=True)
    vn = vc * lax.rsqrt(var + EPS) * lng_ref[...] + lnb_ref[...]
    return u, vn


def _sgu_prompt_kernel(x_ref, sh_ref, sc_ref, g_ref, n0_ref, n1_ref, bin_ref, lng_ref, lnb_ref,
                       ws_ref, bsx_ref, win_in, wout_in, o_ref, win_o, wout_o, wsp_ref, win_s, wout_s,
                       *, chunks, subtiles, lead):
    step = pl.program_id(0)
    n_groups = D_SG // SG_GROUP

    @pl.when(step == 0)
    def _():
        t_idx = lax.broadcasted_iota(jnp.int32, (CHUNK, CHUNK), 0)
        s_idx = lax.broadcasted_iota(jnp.int32, (CHUNK, CHUNK), 1)
        for g in range(n_groups):
            wsp_ref[g] = jnp.where(s_idx <= t_idx, ws_ref[g], 0.0).astype(BF16)

    def body():
        lane = lax.broadcasted_iota(jnp.int32, (1, MXU_N), 1)
        sub_chunks = chunks // subtiles
        rows = sub_chunks * CHUNK
        for s in range(subtiles):
            x = x_ref[s * rows:(s + 1) * rows, :]
            u, vn = _sgu_front(x, sh_ref, sc_ref, n0_ref, win_s, bin_ref, lng_ref, lnb_ref)
            mixed_rows = []
            for c in range(sub_chunks):
                tiles = []
                for c0 in range(0, D_SG, MXU_N):
                    v_tile = vn[c * CHUNK:(c + 1) * CHUNK, c0:c0 + MXU_N]
                    acc = None
                    for g in range(c0 // SG_GROUP, (c0 + MXU_N - 1) // SG_GROUP + 1):
                        in_group = (lane >= g * SG_GROUP - c0) & (lane < (g + 1) * SG_GROUP - c0)
                        part = jnp.dot(wsp_ref[g], jnp.where(in_group, v_tile, 0.0).astype(BF16),
                                       preferred_element_type=F32)
                        acc = part if acc is None else acc + part
                    tiles.append(acc)
                mixed_rows.append(jnp.concatenate(tiles, axis=1) + bsx_ref[...])
            mixed = jnp.concatenate(mixed_rows, axis=0)
            mix = _mm(u * mixed, wout_s[...])
            o_ref[s * rows:(s + 1) * rows, :] = _norm_gated_residual(x, g_ref[...], mix, n1_ref[...])

    pl.when(step < lead)(lambda: _cast_weight_chunks(step, (win_in, wout_in), (win_o, wout_o), (win_s, wout_s)))
    pl.when(step >= lead)(body)


def _sgu_sample_kernel(x_ref, sh_ref, sc_ref, g_ref, n0_ref, n1_ref, bin_ref, lng_ref, lnb_ref,
                       coef_ref, bias_ref, win_ref, wout_ref, o_ref, v_ref, *, steps, batch):
    x = x_ref[...]
    u, vn = _sgu_front(x, sh_ref, sc_ref, n0_ref, win_ref, bin_ref, lng_ref, lnb_ref)
    v_ref[...] = vn
    slabs = []
    for t in range(steps):
        acc = None
        for s in range(t + 1):
            term = coef_ref[t * steps + s:t * steps + s + 1, :] * vn[s * batch:(s + 1) * batch]
            acc = term if acc is None else acc + term
        slabs.append(acc + bias_ref[t:t + 1, :])
    mixed = jnp.concatenate(slabs, axis=0)
    mix = _mm(u * mixed, wout_ref[...])
    o_ref[...] = _norm_gated_residual(x, g_ref[...], mix, n1_ref[...])


def _sgu_vec_specs(layer):
    j = layer // 2
    return [pl.BlockSpec((None, 1, 2 * D_SG), lambda i: (j, 0, 0)),
            pl.BlockSpec((None, 1, D_SG), lambda i: (j, 0, 0)),
            pl.BlockSpec((None, 1, D_SG), lambda i: (j, 0, 0))]


def _sgu_prompt_call(x, mod, norm_g, p, weights, layer, tm, seq_len, name, subtiles=2):
    j = layer // 2
    rows = x.shape[0]
    n_groups = D_SG // SG_GROUP
    lead, w_specs, w_out_specs, w_out_shapes, w_scratch = _weight_plumbing(weights, j)
    row_spec = _row_spec(tm, D_MODEL, lead)
    outs = pl.pallas_call(
        functools.partial(_sgu_prompt_kernel, chunks=tm // CHUNK, subtiles=subtiles, lead=lead),
        grid=(lead + rows // tm,),
        in_specs=[row_spec, *_mod_specs(mod, layer, 0, seq_len // tm, lead), _norm_spec(layer, 0),
                  _norm_spec(layer, 1), *_sgu_vec_specs(layer),
                  pl.BlockSpec((None, n_groups, CHUNK, CHUNK), lambda i: (j, 0, 0, 0)),
                  pl.BlockSpec((None, CHUNK, D_SG), lambda i: (j, 0, 0)), *w_specs],
        out_specs=[row_spec, *w_out_specs],
        out_shape=[jax.ShapeDtypeStruct((rows, D_MODEL), F32), *w_out_shapes],
        scratch_shapes=[pltpu.VMEM((n_groups, CHUNK, CHUNK), BF16), *w_scratch],
        compiler_params=_PARAMS,
        name=name,
    )(x, mod, mod, mod, norm_g, norm_g, p["b_in"], p["ln_g"], p["ln_b"], p["ws"], p["bsx"], *weights)
    return outs[0], tuple(outs[1:])


def _sgu_sample_call(x, mod, norm_g, p, weights, layer, steps, batch, name):
    j = layer // 2
    rows = x.shape[0]
    _, w_specs, _, _, _ = _weight_plumbing(weights, j)
    row_spec = pl.BlockSpec((rows, D_MODEL), lambda i: (0, 0))
    v_spec = pl.BlockSpec((rows, D_SG), lambda i: (0, 0))
    return pl.pallas_call(
        functools.partial(_sgu_sample_kernel, steps=steps, batch=batch),
        grid=(1,),
        in_specs=[row_spec, *_mod_specs(mod, layer, 0, 1, 0), _norm_spec(layer, 0), _norm_spec(layer, 1),
                  *_sgu_vec_specs(layer),
                  pl.BlockSpec((None, steps * steps, D_SG), lambda i: (j, 0, 0)),
                  pl.BlockSpec((None, steps, D_SG), lambda i: (j, 0, 0)), *w_specs],
        out_specs=[row_spec, v_spec],
        out_shape=[jax.ShapeDtypeStruct((rows, D_MODEL), F32), jax.ShapeDtypeStruct((rows, D_SG), F32)],
        compiler_params=_PARAMS,
        name=name,
    )(x, mod, mod, mod, norm_g, norm_g, p["b_in"], p["ln_g"], p["ln_b"], p["coef"], p["bias"], *weights)


def _band_pack(w):
    n_l, n_h, b, _ = w.shape
    eye = jnp.eye(n_h, dtype=w.dtype)
    dense = (w[:, :, :, None, :] * eye[None, :, None, :, None]).reshape(n_l, n_h * b, n_h * b)
    tiles = [dense[:, k0:k0 + BAND_K, c0:c0 + MXU_N] for c0, k0 in _band_starts(n_h * b, b)]
    return jnp.stack(tiles, axis=1).astype(BF16)


def _to_time_major(x):
    n, t, d = x.shape
    return x.transpose(1, 0, 2).reshape(t * n, d)


def _from_time_major(x, n):
    rows, d = x.shape
    return x.reshape(rows // n, n, d).transpose(1, 0, 2)


def kernel(x_prompt, x_sample, c_prompt, c_sample, state_lru_h, state_lru_conv, ada_w, ada_b, norm_g, lru_w_in, lru_conv_w, lru_conv_b, lru_wa, lru_ba, lru_wx, lru_bx, lru_lambda, lru_w_out, sg_w_in, sg_b_in, sg_ln_g, sg_ln_b, sg_ws, sg_bs, sg_w_out, mlp_w_up, mlp_w_down):
    n_p, t_p, _ = x_prompt.shape
    n_s, t_s, _ = x_sample.shape
    n_lru = lru_w_in.shape[0]
    n_sg = sg_w_in.shape[0]

    mod_s, mod_p = _ada_call(jnp.concatenate([c_sample, c_prompt], axis=0), ada_w, ada_b, n_s, n_p)
    mod_p_seq = mod_p.reshape(DEPTH, n_p, 1, 6 * D_MODEL)
    norms = norm_g.reshape(DEPTH * 4, 1, D_MODEL)

    lru = dict(
        conv_w=lru_conv_w, conv_b=lru_conv_b.reshape(n_lru, 1, D_RNN),
        wa=_band_pack(lru_wa), ba=lru_ba.reshape(n_lru, 1, D_RNN),
        wx=_band_pack(lru_wx), bx=lru_bx.reshape(n_lru, 1, D_RNN),
        lam=lru_lambda.reshape(n_lru, 1, D_RNN))
    sgu = dict(
        b_in=sg_b_in.reshape(n_sg, 1, 2 * D_SG),
        ln_g=sg_ln_g.reshape(n_sg, 1, D_SG), ln_b=sg_ln_b.reshape(n_sg, 1, D_SG),
        ws=sg_ws,
        bsx=jnp.repeat(sg_bs.transpose(0, 2, 1), SG_GROUP, axis=2),
        coef=jnp.repeat(sg_ws[:, :, :t_s, :t_s].transpose(0, 2, 3, 1), SG_GROUP, axis=3).reshape(n_sg, t_s * t_s, D_SG),
        bias=jnp.repeat(sg_bs[:, :, :t_s].transpose(0, 2, 1), SG_GROUP, axis=2))

    tm = 512
    steps_p = tm // n_p
    natural = (n_p, t_p, D_MODEL)
    flat = (n_p * t_p, D_MODEL)
    mlp_w = (mlp_w_up, mlp_w_down)
    xp = x_prompt
    xs = _to_time_major(x_sample)
    zero_conv = jnp.zeros(((CONV_W - 1) * n_p, D_RNN), F32)
    zero_h = jnp.zeros((n_p, D_RNN), F32)
    h_p, conv_p, h_s, conv_s, v_s = [], [], [], [], []
    for layer in range(DEPTH):
        j = layer // 2
        if layer % 2 == 0:
            xp, cp, hp, w_mix = _lru_call(xp, mod_p, norms, lru, (lru_w_in, lru_w_out), zero_conv, zero_h,
                                          layer, steps_p, n_p, f"lru_prompt_{layer}")
            xp, w_mlp = _mlp_call(xp, mod_p, norms, mlp_w, layer, tm, natural, n_p, True, 1,
                                  f"mlp_prompt_{layer}")
            xs, cs, hs, _ = _lru_call(xs, mod_s, norms, lru, w_mix, _to_time_major(state_lru_conv[j]),
                                      state_lru_h[j], layer, t_s, n_s, f"lru_sample_{layer}")
            h_p.append(hp)
            conv_p.append(_from_time_major(cp, n_p))
            h_s.append(hs)
            conv_s.append(_from_time_major(cs, n_s))
        else:
            xp, w_mix = _sgu_prompt_call(xp.reshape(flat), mod_p_seq, norms, sgu, (sg_w_in, sg_w_out), layer, tm,
                                         t_p, f"sgu_prompt_{layer}")
            if layer + 1 < DEPTH:
                xp, w_mlp = _mlp_call(xp.reshape(natural), mod_p, norms, mlp_w, layer, tm, flat, n_p, False, 1,
                                      f"mlp_prompt_{layer}")
            else:
                xp, w_mlp = _mlp_call(xp, mod_p_seq, norms, mlp_w, layer, tm, flat, 1, True, t_p // tm,
                                      f"mlp_prompt_{layer}")
            xs, v = _sgu_sample_call(xs, mod_s, norms, sgu, w_mix, layer, t_s, n_s, f"sgu_sample_{layer}")
            v_s.append(_from_time_major(v, n_s))
        xs, _ = _mlp_call(xs, mod_s, norms, w_mlp, layer, t_s * n_s, xs.shape, n_s, True, 1,
                          f"mlp_sample_{layer}")

    y_prompt = xp.reshape(n_p, t_p, D_MODEL)
    y_sample = _from_time_major(xs, n_s)
    return (y_prompt, y_sample, jnp.stack(h_p), jnp.stack(conv_p), jnp.stack(h_s), jnp.stack(conv_s),
            jnp.stack(v_s))
```

```python
import functools

import jax
import jax.numpy as jnp
from jax import lax
from jax.experimental import pallas as pl
from jax.experimental.pallas import tpu as pltpu

D_MODEL = 1024
D_RNN = 1536
D_SG = 1536
D_FF = 4096
DEPTH = 4
CONV_W = 4
LRU_BLOCK = 96
SG_GROUP = 96
CHUNK = 128
LRU_C = 8.0
EPS = 1e-6

MXU_N = 256
LANES = 128
BAND_K = 512
VMEM_LIMIT_BYTES = 56 * 1024 * 1024
CAST_STEPS = 8

F32 = jnp.float32
BF16 = jnp.bfloat16


def _row_rsqrt_ms(x):
    return lax.rsqrt(jnp.mean(x * x, axis=-1, keepdims=True) + EPS)


def _slabs(fn, x, vecs, seq_major):
    b = vecs[0].shape[0]
    if b == 1:
        return fn(x, *vecs)
    rows, d = x.shape
    if seq_major:
        return fn(x.reshape(b, rows // b, d), *[v[:, None, :] for v in vecs]).reshape(rows, d)
    return fn(x.reshape(rows // b, b, d), *[v[None] for v in vecs]).reshape(rows, d)


def _norm_modulate(x, g, sc, sh, seq_major=False):
    return _slabs(lambda xs, scale, shift: xs * scale + shift, x * _row_rsqrt_ms(x), (g * (1.0 + sc), sh), seq_major)


def _norm_gated_residual(x, gate, y, g, seq_major=False):
    yn = _slabs(lambda ys, scale: ys * scale, y * _row_rsqrt_ms(y), (gate * g,), seq_major)
    return x + yn


def _load_rows(x_ref, s, n, time_major):
    if len(x_ref.shape) == 2:
        rows = x_ref.shape[0] // n
        return x_ref[s * rows:(s + 1) * rows, :]
    b, t, d = x_ref.shape
    ts = t // n
    x3 = x_ref[:, s * ts:(s + 1) * ts, :]
    if time_major:
        x3 = jnp.swapaxes(x3, 0, 1)
    return x3.reshape(b * ts, d)


def _store_rows(o_ref, s, n, y, batch, time_major):
    rows, d = y.shape
    if len(o_ref.shape) == 2:
        if batch > 1 and not time_major:
            y = jnp.swapaxes(y.reshape(batch, rows // batch, d), 0, 1).reshape(rows, d)
        o_ref[s * rows:(s + 1) * rows, :] = y
        return
    b, t, _ = o_ref.shape
    ts = t // n
    y3 = jnp.swapaxes(y.reshape(ts, b, d), 0, 1) if time_major else y.reshape(b, ts, d)
    o_ref[:, s * ts:(s + 1) * ts, :] = y3


def _gelu(x):
    k = 0.7978845608028654
    t = jnp.tanh(x * (k + (k * 0.044715) * (x * x)))
    return x * (0.5 + 0.5 * t)


def _mm(a, w):
    return jnp.dot(a.astype(BF16), w, preferred_element_type=F32)


def _band_starts(width, block):
    out = []
    for c0 in range(0, width, MXU_N):
        lo = (c0 // block) * block
        hi = ((c0 + MXU_N - 1) // block + 1) * block
        k0 = min((lo // LANES) * LANES, width - BAND_K)
        assert k0 <= lo and hi <= k0 + BAND_K
        out.append((c0, k0))
    return out


def _block_diag_mm(xb, w_ref, block):
    tiles = [jnp.dot(xb[:, k0:k0 + BAND_K], w_ref[t], preferred_element_type=F32)
             for t, (_, k0) in enumerate(_band_starts(xb.shape[1], block))]
    return jnp.concatenate(tiles, axis=1)


def _cast_weight_chunks(step, srcs, outs, dsts):
    for src, out, dst in zip(srcs, outs, dsts):
        rows = src.shape[0]
        wb = src[...].astype(BF16)
        out[...] = wb
        dst[pl.ds(pl.multiple_of(step * rows, rows), rows), :] = wb


def _resident(shape, index_map):
    return pl.BlockSpec(shape, index_map, pipeline_mode=pl.Buffered(1))


def _tile_index(i, lead):
    return jnp.maximum(i - lead, 0)


def _row_spec(tm, width, lead):
    return pl.BlockSpec((tm, width), lambda i: (_tile_index(i, lead), 0))


def _mod_specs(mod, layer, first, tiles_per_seq, lead):
    if mod.ndim == 4:
        return [pl.BlockSpec((None, None, 1, D_MODEL),
                             lambda i, k=k: (layer, _tile_index(i, lead) // tiles_per_seq, 0, k))
                for k in range(first, first + 3)]
    b = mod.shape[1]
    return [pl.BlockSpec((None, b, D_MODEL), lambda i, k=k: (layer, 0, k)) for k in range(first, first + 3)]


def _norm_spec(layer, which):
    return pl.BlockSpec((None, 1, D_MODEL), lambda i: (4 * layer + which, 0, 0))


def _weight_plumbing(weights, index):
    if weights[0].dtype == BF16:
        return 0, [_resident(w.shape, lambda i: (0, 0)) for w in weights], [], [], []
    in_specs, out_specs, out_shapes, scratch = [], [], [], []
    for w in weights:
        _, r, n = w.shape
        chunk = r // CAST_STEPS
        in_specs.append(pl.BlockSpec((None, chunk, n), lambda i: (index, jnp.minimum(i, CAST_STEPS - 1), 0)))
        out_specs.append(pl.BlockSpec((chunk, n), lambda i: (jnp.minimum(i, CAST_STEPS - 1), 0)))
        out_shapes.append(jax.ShapeDtypeStruct((r, n), BF16))
        scratch.append(pltpu.VMEM((r, n), BF16))
    return CAST_STEPS, in_specs, out_specs, out_shapes, scratch


_PARAMS = pltpu.CompilerParams(dimension_semantics=("arbitrary",), vmem_limit_bytes=VMEM_LIMIT_BYTES)


def _ada_kernel(c_ref, w_ref, b_ref, os_ref, op_ref):
    c = c_ref[...]
    s = c * jax.nn.sigmoid(c)
    m = _mm(s, w_ref[...].astype(BF16)) + b_ref[...]
    n_s = os_ref.shape[0]
    os_ref[...] = m[:n_s]
    op_ref[...] = m[n_s:]


def _ada_call(c_all, ada_w, ada_b, n_s, n_p):
    tn = 1536
    n_all = c_all.shape[0]
    return pl.pallas_call(
        _ada_kernel,
        grid=(DEPTH, 6 * D_MODEL // tn),
        in_specs=[
            pl.BlockSpec((n_all, D_MODEL), lambda l, n: (0, 0)),
            pl.BlockSpec((None, D_MODEL, tn), lambda l, n: (l, 0, n)),
            pl.BlockSpec((None, 1, tn), lambda l, n: (l, 0, n)),
        ],
        out_specs=[
            pl.BlockSpec((None, n_s, tn), lambda l, n: (l, 0, n)),
            pl.BlockSpec((None, n_p, tn), lambda l, n: (l, 0, n)),
        ],
        out_shape=[
            jax.ShapeDtypeStruct((DEPTH, n_s, 6 * D_MODEL), F32),
            jax.ShapeDtypeStruct((DEPTH, n_p, 6 * D_MODEL), F32),
        ],
        compiler_params=pltpu.CompilerParams(
            dimension_semantics=("arbitrary", "arbitrary"), vmem_limit_bytes=VMEM_LIMIT_BYTES),
        name="ada_mod",
    )(c_all, ada_w, ada_b.reshape(DEPTH, 1, 6 * D_MODEL))


def _mlp_kernel(x_ref, sh_ref, sc_ref, g_ref, n2_ref, n3_ref, wup_in, wdn_in, o_ref, *rest,
                batch, time_major, subtiles, lead):
    def body(wup_ref, wdn_ref):
        for s in range(subtiles):
            x = _load_rows(x_ref, s, subtiles, time_major)
            hf = _norm_modulate(x, n2_ref[...], sc_ref[...], sh_ref[...], not time_major)
            hid = jnp.square(jnp.maximum(_mm(hf, wup_ref[...]), 0.0))
            f = _mm(hid, wdn_ref[...])
            y = _norm_gated_residual(x, g_ref[...], f, n3_ref[...], not time_major)
            _store_rows(o_ref, s, subtiles, y, batch, time_major)

    if not lead:
        body(wup_in, wdn_in)
        return
    wup_o, wdn_o, wup_s, wdn_s = rest
    step = pl.program_id(0)
    pl.when(step < lead)(lambda: _cast_weight_chunks(step, (wup_in, wdn_in), (wup_o, wdn_o), (wup_s, wdn_s)))
    pl.when(step >= lead)(lambda: body(wup_s, wdn_s))


def _act_spec(shape, tm, lead):
    if len(shape) == 2:
        return _row_spec(tm, shape[1], lead)
    n, _, d = shape
    return pl.BlockSpec((n, tm // n, d), lambda i: (0, _tile_index(i, lead), 0))


def _mlp_call(x, mod, norm_g, weights, layer, tm, out_shape, batch, time_major, tiles_per_seq, name, subtiles=2):
    n_tiles = x.size // (tm * D_MODEL)
    lead, w_specs, w_out_specs, w_out_shapes, w_scratch = _weight_plumbing(weights, layer)
    outs = pl.pallas_call(
        functools.partial(_mlp_kernel, batch=batch, time_major=time_major, subtiles=subtiles, lead=lead),
        grid=(lead + n_tiles,),
        in_specs=[_act_spec(x.shape, tm, lead), *_mod_specs(mod, layer, 3, tiles_per_seq, lead),
                  _norm_spec(layer, 2), _norm_spec(layer, 3), *w_specs],
        out_specs=[_act_spec(out_shape, tm, lead), *w_out_specs],
        out_shape=[jax.ShapeDtypeStruct(out_shape, F32), *w_out_shapes],
        scratch_shapes=w_scratch,
        compiler_params=_PARAMS,
        name=name,
    )(x, mod, mod, mod, norm_g, norm_g, *weights)
    return outs[0], tuple(outs[1:])


def _lru_kernel(x_ref, sh_ref, sc_ref, g_ref, n0_ref, n1_ref, cw_ref, cb_ref, wa_ref, ba_ref,
                wx_ref, bx_ref, lam_ref, conv0_ref, h0_ref, win_in, wout_in,
                o_ref, convo_ref, ho_ref, *rest, steps, batch, subtiles, lead):
    if lead:
        win_o, wout_o, tail_ref, hc_ref, hs_ref, win_s, wout_s = rest
    else:
        tail_ref, hc_ref, hs_ref = rest
    step = pl.program_id(0)

    @pl.when(step == 0)
    def _():
        tail_ref[...] = conv0_ref[...]
        hc_ref[...] = h0_ref[...]

    def body(win_ref, wout_ref):
        nlam = -lam_ref[...]
        neg_c_softplus = -LRU_C * (jnp.maximum(nlam, 0.0) + jnp.log1p(jnp.exp(-jnp.abs(nlam))))
        cw = cw_ref[...]
        tail = tail_ref[...]
        h = hc_ref[...]
        sub_steps = steps // subtiles
        rows = sub_steps * batch
        for s in range(subtiles):
            r0 = s * rows
            x = _load_rows(x_ref, s, subtiles, True)
            hin = _norm_modulate(x, n0_ref[...], sc_ref[...], sh_ref[...])
            z = _mm(hin, win_ref[...])
            gate_br = z[:, :D_RNN]
            x_br = z[:, D_RNN:]

            xp = jnp.concatenate([tail, x_br], axis=0)
            xc = cb_ref[...]
            for k in range(CONV_W):
                xc = xc + xp[k * batch:k * batch + rows] * cw[k:k + 1]
            tail = xp[rows:]

            xcb = xc.astype(BF16)
            r = jax.nn.sigmoid(_block_diag_mm(xcb, wa_ref, LRU_BLOCK) + ba_ref[...])
            ig = jax.nn.sigmoid(_block_diag_mm(xcb, wx_ref, LRU_BLOCK) + bx_ref[...])
            log_a = r * neg_c_softplus
            a = jnp.exp(log_a)
            q = jnp.tanh(log_a) * (-1.0 - a * a)
            mult = jnp.where(q > 0.0, q * lax.rsqrt(q), 0.0)
            bt = mult * (ig * xc)

            for t in range(sub_steps):
                h = a[t * batch:(t + 1) * batch] * h + bt[t * batch:(t + 1) * batch]
                hs_ref[r0 + t * batch:r0 + (t + 1) * batch, :] = h

            y = hs_ref[r0:r0 + rows, :] * _gelu(gate_br)
            mix = _mm(y, wout_ref[...])
            o_ref[r0:r0 + rows, :] = _norm_gated_residual(x, g_ref[...], mix, n1_ref[...])

        tail_ref[...] = tail
        convo_ref[...] = tail
        hc_ref[...] = h
        ho_ref[...] = h

    if not lead:
        body(win_in, wout_in)
        return
    pl.when(step < lead)(lambda: _cast_weight_chunks(step, (win_in, wout_in), (win_o, wout_o), (win_s, wout_s)))
    pl.when(step >= lead)(lambda: body(win_s, wout_s))


def _lru_call(x, mod, norm_g, p, weights, conv0, h0, layer, steps, batch, name, subtiles=2):
    j = layer // 2
    tm = steps * batch
    rows = x.size // D_MODEL
    tail_rows = (CONV_W - 1) * batch
    lead, w_specs, w_out_specs, w_out_shapes, w_scratch = _weight_plumbing(weights, j)
    row_spec = _row_spec(tm, D_MODEL, lead)
    vec = lambda: pl.BlockSpec((None, 1, D_RNN), lambda i: (j, 0, 0))
    band = lambda: _resident((None, D_RNN // MXU_N, BAND_K, MXU_N), lambda i: (j, 0, 0, 0))
    conv_spec = pl.BlockSpec((tail_rows, D_RNN), lambda i: (0, 0))
    h_spec = pl.BlockSpec((batch, D_RNN), lambda i: (0, 0))
    outs = pl.pallas_call(
        functools.partial(_lru_kernel, steps=steps, batch=batch, subtiles=subtiles, lead=lead),
        grid=(lead + rows // tm,),
        in_specs=[_act_spec(x.shape, tm, lead), *_mod_specs(mod, layer, 0, 1, lead), _norm_spec(layer, 0),
                  _norm_spec(layer, 1),
                  pl.BlockSpec((None, CONV_W, D_RNN), lambda i: (j, 0, 0)), vec(),
                  band(), vec(), band(), vec(), vec(), conv_spec, h_spec, *w_specs],
        out_specs=[row_spec, conv_spec, h_spec, *w_out_specs],
        out_shape=[jax.ShapeDtypeStruct((rows, D_MODEL), F32),
                   jax.ShapeDtypeStruct((tail_rows, D_RNN), F32),
                   jax.ShapeDtypeStruct((batch, D_RNN), F32), *w_out_shapes],
        scratch_shapes=[pltpu.VMEM((tail_rows, D_RNN), F32), pltpu.VMEM((batch, D_RNN), F32),
                        pltpu.VMEM((tm, D_RNN), F32), *w_scratch],
        compiler_params=_PARAMS,
        name=name,
    )(x, mod, mod, mod, norm_g, norm_g, p["conv_w"], p["conv_b"], p["wa"], p["ba"],
      p["wx"], p["bx"], p["lam"], conv0, h0, *weights)
    return outs[0], outs[1], outs[2], tuple(outs[3:])


def _sgu_front(x, sh_ref, sc_ref, n0_ref, win_ref, bin_ref, lng_ref, lnb_ref):
    hin = _norm_modulate(x, n0_ref[...], sc_ref[...], sh_ref[...])
    z = _gelu(_mm(hin, win_ref[...]) + bin_ref[...])
    u = z[:, :D_SG]
    v = z[:, D_SG:]
    vc = v - jnp.mean(v, axis=-1, keepdims=True)
    var = jnp.mean(vc * vc, axis=-1, keepdims=True)
    vn = vc * lax.rsqrt(var + EPS) * lng_ref[...] + lnb_ref[...]
    return u, vn


def _sgu_prompt_kernel(x_ref, sh_ref, sc_ref, g_ref, n0_ref, n1_ref, bin_ref, lng_ref, lnb_ref,
                       ws_ref, bsx_ref, win_in, wout_in, o_ref, win_o, wout_o, wsp_ref, win_s, wout_s,
                       *, chunks, subtiles, lead):
    step = pl.program_id(0)
    n_groups = D_SG // SG_GROUP

    @pl.when(step == 0)
    def _():
        t_idx = lax.broadcasted_iota(jnp.int32, (CHUNK, CHUNK), 0)
        s_idx = lax.broadcasted_iota(jnp.int32, (CHUNK, CHUNK), 1)
        for g in range(n_groups):
            wsp_ref[g] = jnp.where(s_idx <= t_idx, ws_ref[g], 0.0).astype(BF16)

    def body():
        lane = lax.broadcasted_iota(jnp.int32, (CHUNK, MXU_N), 1)
        sub_chunks = chunks // subtiles
        rows = sub_chunks * CHUNK
        for s in range(subtiles):
            x = x_ref[s * rows:(s + 1) * rows, :]
            u, vn = _sgu_front(x, sh_ref, sc_ref, n0_ref, win_s, bin_ref, lng_ref, lnb_ref)
            mixed_rows = []
            for c in range(sub_chunks):
                tiles = []
                for c0 in range(0, D_SG, MXU_N):
                    v_tile = vn[c * CHUNK:(c + 1) * CHUNK, c0:c0 + MXU_N].astype(BF16)
                    acc = None
                    for g in range(c0 // SG_GROUP, (c0 + MXU_N - 1) // SG_GROUP + 1):
                        in_group = (lane >= g * SG_GROUP - c0) & (lane < (g + 1) * SG_GROUP - c0)
                        part = jnp.dot(wsp_ref[g], jnp.where(in_group, v_tile, jnp.zeros_like(v_tile)),
                                       preferred_element_type=F32)
                        acc = part if acc is None else acc + part
                    tiles.append(acc)
                mixed_rows.append(jnp.concatenate(tiles, axis=1) + bsx_ref[...])
            mixed = jnp.concatenate(mixed_rows, axis=0)
            mix = _mm(u * mixed, wout_s[...])
            o_ref[s * rows:(s + 1) * rows, :] = _norm_gated_residual(x, g_ref[...], mix, n1_ref[...])

    pl.when(step < lead)(lambda: _cast_weight_chunks(step, (win_in, wout_in), (win_o, wout_o), (win_s, wout_s)))
    pl.when(step >= lead)(body)


def _sgu_sample_kernel(x_ref, sh_ref, sc_ref, g_ref, n0_ref, n1_ref, bin_ref, lng_ref, lnb_ref,
                       coef_ref, bias_ref, win_ref, wout_ref, o_ref, v_ref, *, steps, batch):
    x = x_ref[...]
    u, vn = _sgu_front(x, sh_ref, sc_ref, n0_ref, win_ref, bin_ref, lng_ref, lnb_ref)
    v_ref[...] = vn
    slabs = []
    for t in range(steps):
        acc = None
        for s in range(t + 1):
            term = coef_ref[t * steps + s:t * steps + s + 1, :] * vn[s * batch:(s + 1) * batch]
            acc = term if acc is None else acc + term
        slabs.append(acc + bias_ref[t:t + 1, :])
    mixed = jnp.concatenate(slabs, axis=0)
    mix = _mm(u * mixed, wout_ref[...])
    o_ref[...] = _norm_gated_residual(x, g_ref[...], mix, n1_ref[...])


def _sgu_vec_specs(layer):
    j = layer // 2
    return [pl.BlockSpec((None, 1, 2 * D_SG), lambda i: (j, 0, 0)),
            pl.BlockSpec((None, 1, D_SG), lambda i: (j, 0, 0)),
            pl.BlockSpec((None, 1, D_SG), lambda i: (j, 0, 0))]


def _sgu_prompt_call(x, mod, norm_g, p, weights, layer, tm, seq_len, name, subtiles=2):
    j = layer // 2
    rows = x.shape[0]
    n_groups = D_SG // SG_GROUP
    lead, w_specs, w_out_specs, w_out_shapes, w_scratch = _weight_plumbing(weights, j)
    row_spec = _row_spec(tm, D_MODEL, lead)
    outs = pl.pallas_call(
        functools.partial(_sgu_prompt_kernel, chunks=tm // CHUNK, subtiles=subtiles, lead=lead),
        grid=(lead + rows // tm,),
        in_specs=[row_spec, *_mod_specs(mod, layer, 0, seq_len // tm, lead), _norm_spec(layer, 0),
                  _norm_spec(layer, 1), *_sgu_vec_specs(layer),
                  pl.BlockSpec((None, n_groups, CHUNK, CHUNK), lambda i: (j, 0, 0, 0)),
                  pl.BlockSpec((None, CHUNK, D_SG), lambda i: (j, 0, 0)), *w_specs],
        out_specs=[row_spec, *w_out_specs],
        out_shape=[jax.ShapeDtypeStruct((rows, D_MODEL), F32), *w_out_shapes],
        scratch_shapes=[pltpu.VMEM((n_groups, CHUNK, CHUNK), BF16), *w_scratch],
        compiler_params=_PARAMS,
        name=name,
    )(x, mod, mod, mod, norm_g, norm_g, p["b_in"], p["ln_g"], p["ln_b"], p["ws"], p["bsx"], *weights)
    return outs[0], tuple(outs[1:])


def _sgu_sample_call(x, mod, norm_g, p, weights, layer, steps, batch, name):
    j = layer // 2
    rows = x.shape[0]
    _, w_specs, _, _, _ = _weight_plumbing(weights, j)
    row_spec = pl.BlockSpec((rows, D_MODEL), lambda i: (0, 0))
    v_spec = pl.BlockSpec((rows, D_SG), lambda i: (0, 0))
    return pl.pallas_call(
        functools.partial(_sgu_sample_kernel, steps=steps, batch=batch),
        grid=(1,),
        in_specs=[row_spec, *_mod_specs(mod, layer, 0, 1, 0), _norm_spec(layer, 0), _norm_spec(layer, 1),
                  *_sgu_vec_specs(layer),
                  pl.BlockSpec((None, steps * steps, D_SG), lambda i: (j, 0, 0)),
                  pl.BlockSpec((None, steps, D_SG), lambda i: (j, 0, 0)), *w_specs],
        out_specs=[row_spec, v_spec],
        out_shape=[jax.ShapeDtypeStruct((rows, D_MODEL), F32), jax.ShapeDtypeStruct((rows, D_SG), F32)],
        compiler_params=_PARAMS,
        name=name,
    )(x, mod, mod, mod, norm_g, norm_g, p["b_in"], p["ln_g"], p["ln_b"], p["coef"], p["bias"], *weights)


def _band_pack(w):
    n_l, n_h, b, _ = w.shape
    width = n_h * b
    rows = w.reshape(n_l, width, b)
    reps = -(-(MXU_N + b) // b)
    tiles = []
    for c0, k0 in _band_starts(width, b):
        band = jnp.concatenate([rows[:, k0:k0 + BAND_K, :]] * reps, axis=2)[:, :, c0 % b:c0 % b + MXU_N]
        r_blk = (k0 + lax.broadcasted_iota(jnp.int32, (BAND_K, MXU_N), 0)) // b
        c_blk = (c0 + lax.broadcasted_iota(jnp.int32, (BAND_K, MXU_N), 1)) // b
        tiles.append(jnp.where(r_blk == c_blk, band, 0.0))
    return jnp.stack(tiles, axis=1).astype(BF16)


def _to_time_major(x):
    n, t, d = x.shape
    return x.transpose(1, 0, 2).reshape(t * n, d)


def _from_time_major(x, n):
    rows, d = x.shape
    return x.reshape(rows // n, n, d).transpose(1, 0, 2)


def kernel(x_prompt, x_sample, c_prompt, c_sample, state_lru_h, state_lru_conv, ada_w, ada_b, norm_g, lru_w_in, lru_conv_w, lru_conv_b, lru_wa, lru_ba, lru_wx, lru_bx, lru_lambda, lru_w_out, sg_w_in, sg_b_in, sg_ln_g, sg_ln_b, sg_ws, sg_bs, sg_w_out, mlp_w_up, mlp_w_down):
    n_p, t_p, _ = x_prompt.shape
    n_s, t_s, _ = x_sample.shape
    n_lru = lru_w_in.shape[0]
    n_sg = sg_w_in.shape[0]

    mod_s, mod_p = _ada_call(jnp.concatenate([c_sample, c_prompt], axis=0), ada_w, ada_b, n_s, n_p)
    mod_p_seq = mod_p.reshape(DEPTH, n_p, 1, 6 * D_MODEL)
    norms = norm_g.reshape(DEPTH * 4, 1, D_MODEL)

    lru = dict(
        conv_w=lru_conv_w, conv_b=lru_conv_b.reshape(n_lru, 1, D_RNN),
        wa=_band_pack(lru_wa), ba=lru_ba.reshape(n_lru, 1, D_RNN),
        wx=_band_pack(lru_wx), bx=lru_bx.reshape(n_lru, 1, D_RNN),
        lam=lru_lambda.reshape(n_lru, 1, D_RNN))
    sgu = dict(
        b_in=sg_b_in.reshape(n_sg, 1, 2 * D_SG),
        ln_g=sg_ln_g.reshape(n_sg, 1, D_SG), ln_b=sg_ln_b.reshape(n_sg, 1, D_SG),
        ws=sg_ws,
        bsx=jnp.repeat(sg_bs.transpose(0, 2, 1), SG_GROUP, axis=2),
        coef=jnp.repeat(sg_ws[:, :, :t_s, :t_s].transpose(0, 2, 3, 1), SG_GROUP, axis=3).reshape(n_sg, t_s * t_s, D_SG),
        bias=jnp.repeat(sg_bs[:, :, :t_s].transpose(0, 2, 1), SG_GROUP, axis=2))

    tm = 512
    steps_p = tm // n_p
    natural = (n_p, t_p, D_MODEL)
    flat = (n_p * t_p, D_MODEL)
    mlp_w = (mlp_w_up, mlp_w_down)
    xp = x_prompt
    xs = _to_time_major(x_sample)
    zero_conv = jnp.zeros(((CONV_W - 1) * n_p, D_RNN), F32)
    zero_h = jnp.zeros((n_p, D_RNN), F32)
    h_p, conv_p, h_s, conv_s, v_s = [], [], [], [], []
    for layer in range(DEPTH):
        j = layer // 2
        if layer % 2 == 0:
            xp, cp, hp, w_mix = _lru_call(xp, mod_p, norms, lru, (lru_w_in, lru_w_out), zero_conv, zero_h,
                                          layer, steps_p, n_p, f"lru_prompt_{layer}")
            xp, w_mlp = _mlp_call(xp, mod_p, norms, mlp_w, layer, tm, natural, n_p, True, 1,
                                  f"mlp_prompt_{layer}")
            xs, cs, hs, _ = _lru_call(xs, mod_s, norms, lru, w_mix, _to_time_major(state_lru_conv[j]),
                                      state_lru_h[j], layer, t_s, n_s, f"lru_sample_{layer}")
            h_p.append(hp)
            conv_p.append(_from_time_major(cp, n_p))
            h_s.append(hs)
            conv_s.append(_from_time_major(cs, n_s))
        else:
            xp, w_mix = _sgu_prompt_call(xp.reshape(flat), mod_p_seq, norms, sgu, (sg_w_in, sg_w_out), layer, tm,
                                         t_p, f"sgu_prompt_{layer}")
            if layer + 1 < DEPTH:
                xp, w_mlp = _mlp_call(xp.reshape(natural), mod_p, norms, mlp_w, layer, tm, flat, n_p, False, 1,
                                      f"mlp_prompt_{layer}")
            else:
                xp, w_mlp = _mlp_call(xp, mod_p_seq, norms, mlp_w, layer, tm, flat, 1, True, t_p // tm,
                                      f"mlp_prompt_{layer}")
            xs, v = _sgu_sample_call(xs, mod_s, norms, sgu, w_mix, layer, t_s, n_s, f"sgu_sample_{layer}")
            v_s.append(_from_time_major(v, n_s))
        xs, _ = _mlp_call(xs, mod_s, norms, w_mlp, layer, t_s * n_s, xs.shape, n_s, True, 1,
                          f"mlp_sample_{layer}")

    y_prompt = xp.reshape(n_p, t_p, D_MODEL)
    y_sample = _from_time_major(xs, n_s)
    return (y_prompt, y_sample, jnp.stack(h_p), jnp.stack(conv_p), jnp.stack(h_s), jnp.stack(conv_s),
            jnp.stack(v_s))
```

```python
import functools

import jax
import jax.numpy as jnp
from jax import lax
from jax.experimental import pallas as pl
from jax.experimental.pallas import tpu as pltpu

D_MODEL = 1024
D_RNN = 1536
D_SG = 1536
D_FF = 4096
DEPTH = 4
CONV_W = 4
LRU_BLOCK = 96
SG_GROUP = 96
CHUNK = 128
LRU_C = 8.0
EPS = 1e-6

MXU_N = 256
LANES = 128
BAND_K = 512
VMEM_LIMIT_BYTES = 58 * 1024 * 1024
CAST_STEPS = 8
TILE_ROWS = 512
SUBTILES = 2

F32 = jnp.float32
BF16 = jnp.bfloat16


def _row_rsqrt_ms(x):
    return lax.rsqrt(jnp.mean(x * x, axis=-1, keepdims=True) + EPS)


def _slabs(fn, x, vecs, seq_major):
    b = vecs[0].shape[0]
    if b == 1:
        return fn(x, *vecs)
    rows, d = x.shape
    if seq_major:
        return fn(x.reshape(b, rows // b, d), *[v[:, None, :] for v in vecs]).reshape(rows, d)
    return fn(x.reshape(rows // b, b, d), *[v[None] for v in vecs]).reshape(rows, d)


def _norm_modulate(x, g, sc, sh, seq_major=False):
    return _slabs(lambda xs, scale, shift: xs * scale + shift, x * _row_rsqrt_ms(x), (g * (1.0 + sc), sh), seq_major)


def _norm_gated_residual(x, gate, y, g, seq_major=False):
    yn = _slabs(lambda ys, scale: ys * scale, y * _row_rsqrt_ms(y), (gate * g,), seq_major)
    return x + yn


def _load_rows(x_ref, s, n, time_major):
    if len(x_ref.shape) == 2:
        rows = x_ref.shape[0] // n
        return x_ref[s * rows:(s + 1) * rows, :]
    b, t, d = x_ref.shape
    ts = t // n
    x3 = x_ref[:, s * ts:(s + 1) * ts, :]
    if time_major:
        x3 = jnp.swapaxes(x3, 0, 1)
    return x3.reshape(b * ts, d)


def _store_rows(o_ref, s, n, y, batch, time_major):
    rows, d = y.shape
    if len(o_ref.shape) == 2:
        if batch > 1 and not time_major:
            y = jnp.swapaxes(y.reshape(batch, rows // batch, d), 0, 1).reshape(rows, d)
        o_ref[s * rows:(s + 1) * rows, :] = y
        return
    b, t, _ = o_ref.shape
    ts = t // n
    y3 = jnp.swapaxes(y.reshape(ts, b, d), 0, 1) if time_major else y.reshape(b, ts, d)
    o_ref[:, s * ts:(s + 1) * ts, :] = y3


def _gelu(x):
    k = 0.7978845608028654
    t = jnp.tanh(x * (k + (k * 0.044715) * (x * x)))
    return x * (0.5 + 0.5 * t)


def _mm(a, w):
    return jnp.dot(a.astype(BF16), w, preferred_element_type=F32)


def _band_starts(width, block):
    out = []
    for c0 in range(0, width, MXU_N):
        lo = (c0 // block) * block
        hi = ((c0 + MXU_N - 1) // block + 1) * block
        k0 = min((lo // LANES) * LANES, width - BAND_K)
        assert k0 <= lo and hi <= k0 + BAND_K
        out.append((c0, k0))
    return out


def _block_diag_mm(xb, w_ref, block):
    tiles = [jnp.dot(xb[:, k0:k0 + BAND_K], w_ref[t], preferred_element_type=F32)
             for t, (_, k0) in enumerate(_band_starts(xb.shape[1], block))]
    return jnp.concatenate(tiles, axis=1)


def _cast_weight_chunks(step, srcs, dsts):
    for src, dst in zip(srcs, dsts):
        rows = src.shape[0]
        dst[pl.ds(pl.multiple_of(step * rows, rows), rows), :] = src[...].astype(BF16)


def _phases(n_tiles, cast, prompt, sample):
    step = pl.program_id(0)
    pl.when(step < CAST_STEPS)(lambda: cast(step))
    pl.when((step >= CAST_STEPS) & (step < CAST_STEPS + n_tiles))(prompt)
    pl.when(step == CAST_STEPS + n_tiles)(sample)


def _const_spec(shape):
    zeros = (0,) * len(shape)
    return pl.BlockSpec(shape, lambda i: zeros, pipeline_mode=pl.Buffered(1))


def _layer_spec(shape, index):
    zeros = (0,) * (len(shape) - 1)
    return pl.BlockSpec((None, *shape[1:]), lambda i: (index, *zeros), pipeline_mode=pl.Buffered(1))


def _tile_index(i, n_tiles):
    return jnp.clip(i - CAST_STEPS, 0, n_tiles - 1)


def _act_spec(shape, n_tiles):
    if len(shape) == 2:
        return pl.BlockSpec((TILE_ROWS, shape[1]), lambda i: (_tile_index(i, n_tiles), 0))
    n, _, d = shape
    return pl.BlockSpec((n, TILE_ROWS // n, d), lambda i: (0, _tile_index(i, n_tiles), 0))


def _mod_specs(mod, layer, first, n_tiles, tiles_per_seq=1):
    if mod.ndim == 4:
        return [pl.BlockSpec((None, None, 1, D_MODEL),
                             lambda i, k=k: (layer, _tile_index(i, n_tiles) // tiles_per_seq, 0, k))
                for k in range(first, first + 3)]
    b = mod.shape[1]
    return [pl.BlockSpec((None, b, D_MODEL), lambda i, k=k: (layer, 0, k)) for k in range(first, first + 3)]


def _norm_spec(layer, which):
    return pl.BlockSpec((None, 1, D_MODEL), lambda i: (4 * layer + which, 0, 0))


def _streamed_weights(weights, index):
    in_specs, scratch = [], []
    for w in weights:
        _, r, n = w.shape
        in_specs.append(pl.BlockSpec((None, r // CAST_STEPS, n),
                                     lambda i: (index, jnp.minimum(i, CAST_STEPS - 1), 0)))
        scratch.append(pltpu.VMEM((r, n), BF16))
    return in_specs, scratch


_PARAMS = pltpu.CompilerParams(dimension_semantics=("arbitrary",), vmem_limit_bytes=VMEM_LIMIT_BYTES)


def _ada_kernel(c_ref, w_ref, b_ref, os_ref, op_ref):
    c = c_ref[...]
    s = c * jax.nn.sigmoid(c)
    m = _mm(s, w_ref[...].astype(BF16)) + b_ref[...]
    n_s = os_ref.shape[0]
    os_ref[...] = m[:n_s]
    op_ref[...] = m[n_s:]


def _ada_call(c_all, ada_w, ada_b, n_s, n_p):
    tn = 1536
    n_all = c_all.shape[0]
    return pl.pallas_call(
        _ada_kernel,
        grid=(DEPTH, 6 * D_MODEL // tn),
        in_specs=[
            pl.BlockSpec((n_all, D_MODEL), lambda l, n: (0, 0)),
            pl.BlockSpec((None, D_MODEL, tn), lambda l, n: (l, 0, n)),
            pl.BlockSpec((None, 1, tn), lambda l, n: (l, 0, n)),
        ],
        out_specs=[
            pl.BlockSpec((None, n_s, tn), lambda l, n: (l, 0, n)),
            pl.BlockSpec((None, n_p, tn), lambda l, n: (l, 0, n)),
        ],
        out_shape=[
            jax.ShapeDtypeStruct((DEPTH, n_s, 6 * D_MODEL), F32),
            jax.ShapeDtypeStruct((DEPTH, n_p, 6 * D_MODEL), F32),
        ],
        compiler_params=pltpu.CompilerParams(
            dimension_semantics=("arbitrary", "arbitrary"), vmem_limit_bytes=VMEM_LIMIT_BYTES),
        name="ada_mod",
    )(c_all, ada_w, ada_b.reshape(DEPTH, 1, 6 * D_MODEL))


def _mlp_rows(x_ref, sh_ref, sc_ref, g_ref, o_ref, n2_ref, n3_ref, wup_ref, wdn_ref, batch, time_major):
    for s in range(SUBTILES):
        x = _load_rows(x_ref, s, SUBTILES, time_major)
        hf = _norm_modulate(x, n2_ref[...], sc_ref[...], sh_ref[...], not time_major)
        hid = jnp.square(jnp.maximum(_mm(hf, wup_ref[...]), 0.0))
        f = _mm(hid, wdn_ref[...])
        y = _norm_gated_residual(x, g_ref[...], f, n3_ref[...], not time_major)
        _store_rows(o_ref, s, SUBTILES, y, batch, time_major)


def _mlp_kernel(xp_ref, shp_ref, scp_ref, gp_ref, xs_ref, shs_ref, scs_ref, gs_ref, n2_ref, n3_ref,
                wup_in, wdn_in, op_ref, os_ref, wup_s, wdn_s, *, n_tiles, batch_p, time_major_p, batch_s):
    _phases(
        n_tiles,
        lambda step: _cast_weight_chunks(step, (wup_in, wdn_in), (wup_s, wdn_s)),
        lambda: _mlp_rows(xp_ref, shp_ref, scp_ref, gp_ref, op_ref, n2_ref, n3_ref, wup_s, wdn_s,
                          batch_p, time_major_p),
        lambda: _mlp_rows(xs_ref, shs_ref, scs_ref, gs_ref, os_ref, n2_ref, n3_ref, wup_s, wdn_s,
                          batch_s, True))


def _mlp_call(xp, mod_p, xs, mod_s, norm_g, weights, layer, out_shape, batch_p, time_major_p, tiles_per_seq, name):
    n_tiles = xp.size // (TILE_ROWS * D_MODEL)
    w_specs, w_scratch = _streamed_weights(weights, layer)
    return pl.pallas_call(
        functools.partial(_mlp_kernel, n_tiles=n_tiles, batch_p=batch_p, time_major_p=time_major_p,
                          batch_s=mod_s.shape[1]),
        grid=(CAST_STEPS + n_tiles + 1,),
        in_specs=[_act_spec(xp.shape, n_tiles), *_mod_specs(mod_p, layer, 3, n_tiles, tiles_per_seq),
                  _const_spec(xs.shape), *_mod_specs(mod_s, layer, 3, n_tiles),
                  _norm_spec(layer, 2), _norm_spec(layer, 3), *w_specs],
        out_specs=[_act_spec(out_shape, n_tiles), pl.BlockSpec(xs.shape, lambda i: (0, 0))],
        out_shape=[jax.ShapeDtypeStruct(out_shape, F32), jax.ShapeDtypeStruct(xs.shape, F32)],
        scratch_shapes=w_scratch,
        compiler_params=_PARAMS,
        name=name,
    )(xp, mod_p, mod_p, mod_p, xs, mod_s, mod_s, mod_s, norm_g, norm_g, *weights)


def _lru_rows(x_ref, sh_ref, sc_ref, g_ref, o_ref, tail, h, p, hs_ref, steps, batch):
    n0_ref, n1_ref, cw_ref, cb_ref, wa_ref, ba_ref, wx_ref, bx_ref, lam_ref, win_ref, wout_ref = p
    nlam = -lam_ref[...]
    neg_c_softplus = -LRU_C * (jnp.maximum(nlam, 0.0) + jnp.log1p(jnp.exp(-jnp.abs(nlam))))
    cw = cw_ref[...]
    sub_steps = steps // SUBTILES
    rows = sub_steps * batch
    for s in range(SUBTILES):
        r0 = s * rows
        x = _load_rows(x_ref, s, SUBTILES, True)
        hin = _norm_modulate(x, n0_ref[...], sc_ref[...], sh_ref[...])
        z = _mm(hin, win_ref[...])
        gate_br = z[:, :D_RNN]
        x_br = z[:, D_RNN:]

        xp = jnp.concatenate([tail, x_br], axis=0)
        xc = cb_ref[...]
        for k in range(CONV_W):
            xc = xc + xp[k * batch:k * batch + rows] * cw[k:k + 1]
        tail = xp[rows:]

        xcb = xc.astype(BF16)
        r = jax.nn.sigmoid(_block_diag_mm(xcb, wa_ref, LRU_BLOCK) + ba_ref[...])
        ig = jax.nn.sigmoid(_block_diag_mm(xcb, wx_ref, LRU_BLOCK) + bx_ref[...])
        log_a = r * neg_c_softplus
        a = jnp.exp(log_a)
        q = jnp.tanh(log_a) * (-1.0 - a * a)
        mult = jnp.where(q > 0.0, q * lax.rsqrt(q), 0.0)
        bt = mult * (ig * xc)

        for t in range(sub_steps):
            h = a[t * batch:(t + 1) * batch] * h + bt[t * batch:(t + 1) * batch]
            hs_ref[r0 + t * batch:r0 + (t + 1) * batch, :] = h

        y = hs_ref[r0:r0 + rows, :] * _gelu(gate_br)
        mix = _mm(y, wout_ref[...])
        o_ref[r0:r0 + rows, :] = _norm_gated_residual(x, g_ref[...], mix, n1_ref[...])
    return tail, h


def _lru_kernel(xp_ref, shp_ref, scp_ref, gp_ref, xs_ref, shs_ref, scs_ref, gs_ref, n0_ref, n1_ref,
                cw_ref, cb_ref, wa_ref, ba_ref, wx_ref, bx_ref, lam_ref, conv0p_ref, h0p_ref, conv0s_ref, h0s_ref,
                win_in, wout_in, op_ref, convp_ref, hp_ref, os_ref, convs_ref, hso_ref,
                tail_ref, hc_ref, hs_ref, win_s, wout_s, *, n_tiles, steps_p, batch_p, steps_s, batch_s):
    p = (n0_ref, n1_ref, cw_ref, cb_ref, wa_ref, ba_ref, wx_ref, bx_ref, lam_ref, win_s, wout_s)

    @pl.when(pl.program_id(0) == 0)
    def _():
        tail_ref[...] = conv0p_ref[...]
        hc_ref[...] = h0p_ref[...]

    def prompt():
        tail, h = _lru_rows(xp_ref, shp_ref, scp_ref, gp_ref, op_ref, tail_ref[...], hc_ref[...], p, hs_ref,
                            steps_p, batch_p)
        tail_ref[...] = tail
        convp_ref[...] = tail
        hc_ref[...] = h
        hp_ref[...] = h

    def sample():
        tail, h = _lru_rows(xs_ref, shs_ref, scs_ref, gs_ref, os_ref, conv0s_ref[...], h0s_ref[...], p, hs_ref,
                            steps_s, batch_s)
        convs_ref[...] = tail
        hso_ref[...] = h

    _phases(n_tiles, lambda step: _cast_weight_chunks(step, (win_in, wout_in), (win_s, wout_s)), prompt, sample)


def _lru_call(xp, mod_p, xs, mod_s, norm_g, p, weights, conv0_p, h0_p, conv0_s, h0_s, layer, name):
    j = layer // 2
    batch_p, batch_s = h0_p.shape[0], h0_s.shape[0]
    rows_s = xs.shape[0]
    n_tiles = xp.size // (TILE_ROWS * D_MODEL)
    w_specs, w_scratch = _streamed_weights(weights, j)
    row_spec = pl.BlockSpec((TILE_ROWS, D_MODEL), lambda i: (_tile_index(i, n_tiles), 0))
    whole = lambda a: pl.BlockSpec(a.shape, lambda i: (0, 0))
    return pl.pallas_call(
        functools.partial(_lru_kernel, n_tiles=n_tiles, steps_p=TILE_ROWS // batch_p, batch_p=batch_p,
                          steps_s=rows_s // batch_s, batch_s=batch_s),
        grid=(CAST_STEPS + n_tiles + 1,),
        in_specs=[_act_spec(xp.shape, n_tiles), *_mod_specs(mod_p, layer, 0, n_tiles),
                  _const_spec(xs.shape), *_mod_specs(mod_s, layer, 0, n_tiles),
                  _norm_spec(layer, 0), _norm_spec(layer, 1),
                  _layer_spec(p["conv_w"].shape, j), _layer_spec(p["conv_b"].shape, j),
                  _layer_spec(p["wa"].shape, j), _layer_spec(p["ba"].shape, j),
                  _layer_spec(p["wx"].shape, j), _layer_spec(p["bx"].shape, j), _layer_spec(p["lam"].shape, j),
                  _const_spec(conv0_p.shape), _const_spec(h0_p.shape),
                  _const_spec(conv0_s.shape), _const_spec(h0_s.shape), *w_specs],
        out_specs=[row_spec, whole(conv0_p), whole(h0_p), whole(xs), whole(conv0_s), whole(h0_s)],
        out_shape=[jax.ShapeDtypeStruct((xp.size // D_MODEL, D_MODEL), F32),
                   jax.ShapeDtypeStruct(conv0_p.shape, F32), jax.ShapeDtypeStruct(h0_p.shape, F32),
                   jax.ShapeDtypeStruct(xs.shape, F32),
                   jax.ShapeDtypeStruct(conv0_s.shape, F32), jax.ShapeDtypeStruct(h0_s.shape, F32)],
        scratch_shapes=[pltpu.VMEM(conv0_p.shape, F32), pltpu.VMEM(h0_p.shape, F32),
                        pltpu.VMEM((max(TILE_ROWS, rows_s), D_RNN), F32), *w_scratch],
        compiler_params=_PARAMS,
        name=name,
    )(xp, mod_p, mod_p, mod_p, xs, mod_s, mod_s, mod_s, norm_g, norm_g, p["conv_w"], p["conv_b"],
      p["wa"], p["ba"], p["wx"], p["bx"], p["lam"], conv0_p, h0_p, conv0_s, h0_s, *weights)


def _sgu_front(x, sh_ref, sc_ref, n0_ref, win_ref, bin_ref, lng_ref, lnb_ref):
    hin = _norm_modulate(x, n0_ref[...], sc_ref[...], sh_ref[...])
    z = _gelu(_mm(hin, win_ref[...]) + bin_ref[...])
    u = z[:, :D_SG]
    v = z[:, D_SG:]
    vc = v - jnp.mean(v, axis=-1, keepdims=True)
    var = jnp.mean(vc * vc, axis=-1, keepdims=True)
    vn = vc * lax.rsqrt(var + EPS) * lng_ref[...] + lnb_ref[...]
    return u, vn


def _sgu_prompt_rows(x_ref, sh_ref, sc_ref, g_ref, o_ref, front, n1_ref, wsp_ref, bsx_ref, wout_ref):
    lane = lax.broadcasted_iota(jnp.int32, (CHUNK, MXU_N), 1)
    rows = x_ref.shape[0] // SUBTILES
    for s in range(SUBTILES):
        x = x_ref[s * rows:(s + 1) * rows, :]
        u, vn = _sgu_front(x, sh_ref, sc_ref, *front)
        mixed_rows = []
        for c in range(rows // CHUNK):
            tiles = []
            for c0 in range(0, D_SG, MXU_N):
                v_tile = vn[c * CHUNK:(c + 1) * CHUNK, c0:c0 + MXU_N].astype(BF16)
                acc = None
                for g in range(c0 // SG_GROUP, (c0 + MXU_N - 1) // SG_GROUP + 1):
                    in_group = (lane >= g * SG_GROUP - c0) & (lane < (g + 1) * SG_GROUP - c0)
                    part = jnp.dot(wsp_ref[g], jnp.where(in_group, v_tile, jnp.zeros_like(v_tile)),
                                   preferred_element_type=F32)
                    acc = part if acc is None else acc + part
                tiles.append(acc)
            mixed_rows.append(jnp.concatenate(tiles, axis=1) + bsx_ref[...])
        mixed = jnp.concatenate(mixed_rows, axis=0)
        mix = _mm(u * mixed, wout_ref[...])
        o_ref[s * rows:(s + 1) * rows, :] = _norm_gated_residual(x, g_ref[...], mix, n1_ref[...])


def _sgu_sample_rows(x_ref, sh_ref, sc_ref, g_ref, o_ref, v_ref, front, n1_ref, coef_ref, bias_ref, wout_ref, batch):
    x = x_ref[...]
    steps = x.shape[0] // batch
    u, vn = _sgu_front(x, sh_ref, sc_ref, *front)
    v_ref[...] = vn
    slabs = []
    for t in range(steps):
        acc = None
        for s in range(t + 1):
            term = coef_ref[t * steps + s:t * steps + s + 1, :] * vn[s * batch:(s + 1) * batch]
            acc = term if acc is None else acc + term
        slabs.append(acc + bias_ref[t:t + 1, :])
    mixed = jnp.concatenate(slabs, axis=0)
    mix = _mm(u * mixed, wout_ref[...])
    o_ref[...] = _norm_gated_residual(x, g_ref[...], mix, n1_ref[...])


def _sgu_kernel(xp_ref, shp_ref, scp_ref, gp_ref, xs_ref, shs_ref, scs_ref, gs_ref, n0_ref, n1_ref,
                bin_ref, lng_ref, lnb_ref, ws_ref, bsx_ref, coef_ref, bias_ref, win_in, wout_in,
                op_ref, os_ref, vs_ref, wsp_ref, win_s, wout_s, *, n_tiles, batch_s):
    front = (n0_ref, win_s, bin_ref, lng_ref, lnb_ref)

    @pl.when(pl.program_id(0) == 0)
    def _():
        t_idx = lax.broadcasted_iota(jnp.int32, (CHUNK, CHUNK), 0)
        s_idx = lax.broadcasted_iota(jnp.int32, (CHUNK, CHUNK), 1)
        for g in range(D_SG // SG_GROUP):
            wsp_ref[g] = jnp.where(s_idx <= t_idx, ws_ref[g], 0.0).astype(BF16)

    _phases(
        n_tiles,
        lambda step: _cast_weight_chunks(step, (win_in, wout_in), (win_s, wout_s)),
        lambda: _sgu_prompt_rows(xp_ref, shp_ref, scp_ref, gp_ref, op_ref, front, n1_ref, wsp_ref, bsx_ref, wout_s),
        lambda: _sgu_sample_rows(xs_ref, shs_ref, scs_ref, gs_ref, os_ref, vs_ref, front, n1_ref, coef_ref,
                                 bias_ref, wout_s, batch_s))


def _sgu_call(xp, mod_p, xs, mod_s, norm_g, p, weights, layer, seq_len, name):
    j = layer // 2
    n_tiles = xp.shape[0] // TILE_ROWS
    w_specs, w_scratch = _streamed_weights(weights, j)
    v_shape = (xs.shape[0], D_SG)
    return pl.pallas_call(
        functools.partial(_sgu_kernel, n_tiles=n_tiles, batch_s=mod_s.shape[1]),
        grid=(CAST_STEPS + n_tiles + 1,),
        in_specs=[_act_spec(xp.shape, n_tiles), *_mod_specs(mod_p, layer, 0, n_tiles, seq_len // TILE_ROWS),
                  _const_spec(xs.shape), *_mod_specs(mod_s, layer, 0, n_tiles),
                  _norm_spec(layer, 0), _norm_spec(layer, 1),
                  _layer_spec(p["b_in"].shape, j), _layer_spec(p["ln_g"].shape, j), _layer_spec(p["ln_b"].shape, j),
                  _layer_spec(p["ws"].shape, j), _layer_spec(p["bsx"].shape, j),
                  _layer_spec(p["coef"].shape, j), _layer_spec(p["bias"].shape, j), *w_specs],
        out_specs=[_act_spec(xp.shape, n_tiles), pl.BlockSpec(xs.shape, lambda i: (0, 0)),
                   pl.BlockSpec(v_shape, lambda i: (0, 0))],
        out_shape=[jax.ShapeDtypeStruct(xp.shape, F32), jax.ShapeDtypeStruct(xs.shape, F32),
                   jax.ShapeDtypeStruct(v_shape, F32)],
        scratch_shapes=[pltpu.VMEM((D_SG // SG_GROUP, CHUNK, CHUNK), BF16), *w_scratch],
        compiler_params=_PARAMS,
        name=name,
    )(xp, mod_p, mod_p, mod_p, xs, mod_s, mod_s, mod_s, norm_g, norm_g, p["b_in"], p["ln_g"], p["ln_b"],
      p["ws"], p["bsx"], p["coef"], p["bias"], *weights)


def _band_pack(w):
    n_l, n_h, b, _ = w.shape
    width = n_h * b
    rows = w.reshape(n_l, width, b)
    reps = -(-(MXU_N + b) // b)
    tiles = []
    for c0, k0 in _band_starts(width, b):
        band = jnp.concatenate([rows[:, k0:k0 + BAND_K, :]] * reps, axis=2)[:, :, c0 % b:c0 % b + MXU_N]
        r_blk = (k0 + lax.broadcasted_iota(jnp.int32, (BAND_K, MXU_N), 0)) // b
        c_blk = (c0 + lax.broadcasted_iota(jnp.int32, (BAND_K, MXU_N), 1)) // b
        tiles.append(jnp.where(r_blk == c_blk, band, 0.0))
    return jnp.stack(tiles, axis=1).astype(BF16)


def _to_time_major(x):
    n, t, d = x.shape
    return x.transpose(1, 0, 2).reshape(t * n, d)


def _from_time_major(x, n):
    rows, d = x.shape
    return x.reshape(rows // n, n, d).transpose(1, 0, 2)


def kernel(x_prompt, x_sample, c_prompt, c_sample, state_lru_h, state_lru_conv, ada_w, ada_b, norm_g, lru_w_in, lru_conv_w, lru_conv_b, lru_wa, lru_ba, lru_wx, lru_bx, lru_lambda, lru_w_out, sg_w_in, sg_b_in, sg_ln_g, sg_ln_b, sg_ws, sg_bs, sg_w_out, mlp_w_up, mlp_w_down):
    n_p, t_p, _ = x_prompt.shape
    n_s, t_s, _ = x_sample.shape
    n_lru = lru_w_in.shape[0]
    n_sg = sg_w_in.shape[0]

    mod_s, mod_p = _ada_call(jnp.concatenate([c_sample, c_prompt], axis=0), ada_w, ada_b, n_s, n_p)
    mod_p_seq = mod_p.reshape(DEPTH, n_p, 1, 6 * D_MODEL)
    norms = norm_g.reshape(DEPTH * 4, 1, D_MODEL)

    lru = dict(
        conv_w=lru_conv_w, conv_b=lru_conv_b.reshape(n_lru, 1, D_RNN),
        wa=_band_pack(lru_wa), ba=lru_ba.reshape(n_lru, 1, D_RNN),
        wx=_band_pack(lru_wx), bx=lru_bx.reshape(n_lru, 1, D_RNN),
        lam=lru_lambda.reshape(n_lru, 1, D_RNN))
    sgu = dict(
        b_in=sg_b_in.reshape(n_sg, 1, 2 * D_SG),
        ln_g=sg_ln_g.reshape(n_sg, 1, D_SG), ln_b=sg_ln_b.reshape(n_sg, 1, D_SG),
        ws=sg_ws,
        bsx=jnp.repeat(sg_bs.transpose(0, 2, 1), SG_GROUP, axis=2),
        coef=jnp.repeat(sg_ws[:, :, :t_s, :t_s].transpose(0, 2, 3, 1), SG_GROUP, axis=3).reshape(n_sg, t_s * t_s, D_SG),
        bias=jnp.repeat(sg_bs[:, :, :t_s].transpose(0, 2, 1), SG_GROUP, axis=2))

    natural = (n_p, t_p, D_MODEL)
    flat = (n_p * t_p, D_MODEL)
    mlp_w = (mlp_w_up, mlp_w_down)
    xp = x_prompt
    xs = _to_time_major(x_sample)
    zero_conv = jnp.zeros(((CONV_W - 1) * n_p, D_RNN), F32)
    zero_h = jnp.zeros((n_p, D_RNN), F32)
    h_p, conv_p, h_s, conv_s, v_s = [], [], [], [], []
    for layer in range(DEPTH):
        j = layer // 2
        if layer % 2 == 0:
            xp, cp, hp, xs, cs, hs = _lru_call(
                xp, mod_p, xs, mod_s, norms, lru, (lru_w_in, lru_w_out), zero_conv, zero_h,
                _to_time_major(state_lru_conv[j]), state_lru_h[j], layer, f"lru_{layer}")
            xp, xs = _mlp_call(xp, mod_p, xs, mod_s, norms, mlp_w, layer, natural, n_p, True, 1, f"mlp_{layer}")
            h_p.append(hp)
            conv_p.append(_from_time_major(cp, n_p))
            h_s.append(hs)
            conv_s.append(_from_time_major(cs, n_s))
        else:
            xp, xs, v = _sgu_call(xp.reshape(flat), mod_p_seq, xs, mod_s, norms, sgu, (sg_w_in, sg_w_out), layer,
                                  t_p, f"sgu_{layer}")
            v_s.append(_from_time_major(v, n_s))
            if layer + 1 < DEPTH:
                xp, xs = _mlp_call(xp.reshape(natural), mod_p, xs, mod_s, norms, mlp_w, layer, flat, n_p, False, 1,
                                   f"mlp_{layer}")
            else:
                xp, xs = _mlp_call(xp, mod_p_seq, xs, mod_s, norms, mlp_w, layer, flat, 1, True, t_p // TILE_ROWS,
                                   f"mlp_{layer}")

    y_prompt = xp.reshape(n_p, t_p, D_MODEL)
    y_sample = _from_time_major(xs, n_s)
    return (y_prompt, y_sample, jnp.stack(h_p), jnp.stack(conv_p), jnp.stack(h_s), jnp.stack(conv_s),
            jnp.stack(v_s))
```

```python
import functools

import jax
import jax.numpy as jnp
from jax import lax
from jax.experimental import pallas as pl
from jax.experimental.pallas import tpu as pltpu

D_MODEL = 1024
D_RNN = 1536
D_SG = 1536
D_FF = 4096
DEPTH = 4
CONV_W = 4
LRU_BLOCK = 96
SG_GROUP = 96
CHUNK = 128
LRU_C = 8.0
EPS = 1e-6

MXU_N = 256
LANES = 128
BAND_K = 512
VMEM_LIMIT_BYTES = 58 * 1024 * 1024
CAST_STEPS = 8
SUB_ROWS = 256
MLP_TILE_ROWS = 1024
SGU_TILE_ROWS = 1024
LRU_TILE_ROWS = 512

F32 = jnp.float32
BF16 = jnp.bfloat16


def _row_rsqrt_ms(x):
    return lax.rsqrt(jnp.mean(x * x, axis=-1, keepdims=True) + EPS)


def _slabs(fn, x, vecs, seq_major):
    b = vecs[0].shape[0]
    if b == 1:
        return fn(x, *vecs)
    rows, d = x.shape
    if seq_major:
        return fn(x.reshape(b, rows // b, d), *[v[:, None, :] for v in vecs]).reshape(rows, d)
    return fn(x.reshape(rows // b, b, d), *[v[None] for v in vecs]).reshape(rows, d)


def _norm_modulate(x, g, sc, sh, seq_major=False):
    return _slabs(lambda xs, scale, shift: xs * scale + shift, x * _row_rsqrt_ms(x), (g * (1.0 + sc), sh), seq_major)


def _norm_gated_residual(x, gate, y, g, seq_major=False):
    yn = _slabs(lambda ys, scale: ys * scale, y * _row_rsqrt_ms(y), (gate * g,), seq_major)
    return x + yn


def _load_rows(x_ref, s, n, time_major):
    if len(x_ref.shape) == 2:
        rows = x_ref.shape[0] // n
        return x_ref[s * rows:(s + 1) * rows, :]
    b, t, d = x_ref.shape
    ts = t // n
    x3 = x_ref[:, s * ts:(s + 1) * ts, :]
    if time_major:
        x3 = jnp.swapaxes(x3, 0, 1)
    return x3.reshape(b * ts, d)


def _store_rows(o_ref, s, n, y, batch, time_major):
    rows, d = y.shape
    if len(o_ref.shape) == 2:
        if batch > 1 and not time_major:
            y = jnp.swapaxes(y.reshape(batch, rows // batch, d), 0, 1).reshape(rows, d)
        o_ref[s * rows:(s + 1) * rows, :] = y
        return
    b, t, _ = o_ref.shape
    ts = t // n
    y3 = jnp.swapaxes(y.reshape(ts, b, d), 0, 1) if time_major else y.reshape(b, ts, d)
    o_ref[:, s * ts:(s + 1) * ts, :] = y3


def _gelu(x):
    k = 0.7978845608028654
    t = jnp.tanh(x * (k + (k * 0.044715) * (x * x)))
    return x * (0.5 + 0.5 * t)


def _mm(a, w):
    return jnp.dot(a.astype(BF16), w, preferred_element_type=F32)


def _band_starts(width, block):
    out = []
    for c0 in range(0, width, MXU_N):
        lo = (c0 // block) * block
        hi = ((c0 + MXU_N - 1) // block + 1) * block
        k0 = min((lo // LANES) * LANES, width - BAND_K)
        assert k0 <= lo and hi <= k0 + BAND_K
        out.append((c0, k0))
    return out


def _block_diag_mm(xb, w_ref, block):
    tiles = [jnp.dot(xb[:, k0:k0 + BAND_K], w_ref[t], preferred_element_type=F32)
             for t, (_, k0) in enumerate(_band_starts(xb.shape[1], block))]
    return jnp.concatenate(tiles, axis=1)


def _cast_weight_chunks(step, srcs, dsts):
    for src, dst in zip(srcs, dsts):
        rows = src.shape[0]
        dst[pl.ds(pl.multiple_of(step * rows, rows), rows), :] = src[...].astype(BF16)


def _phases(n_tiles, cast, prompt, sample):
    step = pl.program_id(0)
    pl.when(step < CAST_STEPS)(lambda: cast(step))
    pl.when((step >= CAST_STEPS) & (step < CAST_STEPS + n_tiles))(prompt)
    pl.when(step == CAST_STEPS + n_tiles)(sample)


def _const_spec(shape):
    zeros = (0,) * len(shape)
    return pl.BlockSpec(shape, lambda i: zeros, pipeline_mode=pl.Buffered(1))


def _layer_spec(shape, index):
    zeros = (0,) * (len(shape) - 1)
    return pl.BlockSpec((None, *shape[1:]), lambda i: (index, *zeros), pipeline_mode=pl.Buffered(1))


def _tile_index(i, n_tiles):
    return jnp.clip(i - CAST_STEPS, 0, n_tiles - 1)


def _subtiles(rows):
    return max(rows // SUB_ROWS, 1)


def _act_spec(shape, tile_rows):
    n_tiles = (shape[0] if len(shape) == 2 else shape[0] * shape[1]) // tile_rows
    if len(shape) == 2:
        return pl.BlockSpec((tile_rows, shape[1]), lambda i: (_tile_index(i, n_tiles), 0))
    n, _, d = shape
    return pl.BlockSpec((n, tile_rows // n, d), lambda i: (0, _tile_index(i, n_tiles), 0))


def _mod_specs(mod, layer, first, n_tiles, tiles_per_seq=1):
    if mod.ndim == 4:
        return [pl.BlockSpec((None, None, 1, D_MODEL),
                             lambda i, k=k: (layer, _tile_index(i, n_tiles) // tiles_per_seq, 0, k))
                for k in range(first, first + 3)]
    b = mod.shape[1]
    return [pl.BlockSpec((None, b, D_MODEL), lambda i, k=k: (layer, 0, k)) for k in range(first, first + 3)]


def _norm_spec(layer, which):
    return pl.BlockSpec((None, 1, D_MODEL), lambda i: (4 * layer + which, 0, 0))


def _streamed_weights(weights, index):
    in_specs, scratch = [], []
    for w in weights:
        _, r, n = w.shape
        in_specs.append(pl.BlockSpec((None, r // CAST_STEPS, n),
                                     lambda i: (index, jnp.minimum(i, CAST_STEPS - 1), 0)))
        scratch.append(pltpu.VMEM((r, n), BF16))
    return in_specs, scratch


_PARAMS = pltpu.CompilerParams(dimension_semantics=("arbitrary",), vmem_limit_bytes=VMEM_LIMIT_BYTES)


def _ada_kernel(c_ref, w_ref, b_ref, os_ref, op_ref):
    c = c_ref[...]
    s = c * jax.nn.sigmoid(c)
    m = _mm(s, w_ref[...].astype(BF16)) + b_ref[...]
    n_s = os_ref.shape[0]
    os_ref[...] = m[:n_s]
    op_ref[...] = m[n_s:]


def _ada_call(c_all, ada_w, ada_b, n_s, n_p):
    tn = 1536
    n_all = c_all.shape[0]
    return pl.pallas_call(
        _ada_kernel,
        grid=(DEPTH, 6 * D_MODEL // tn),
        in_specs=[
            pl.BlockSpec((n_all, D_MODEL), lambda l, n: (0, 0)),
            pl.BlockSpec((None, D_MODEL, tn), lambda l, n: (l, 0, n)),
            pl.BlockSpec((None, 1, tn), lambda l, n: (l, 0, n)),
        ],
        out_specs=[
            pl.BlockSpec((None, n_s, tn), lambda l, n: (l, 0, n)),
            pl.BlockSpec((None, n_p, tn), lambda l, n: (l, 0, n)),
        ],
        out_shape=[
            jax.ShapeDtypeStruct((DEPTH, n_s, 6 * D_MODEL), F32),
            jax.ShapeDtypeStruct((DEPTH, n_p, 6 * D_MODEL), F32),
        ],
        compiler_params=pltpu.CompilerParams(
            dimension_semantics=("arbitrary", "arbitrary"), vmem_limit_bytes=VMEM_LIMIT_BYTES),
        name="ada_mod",
    )(c_all, ada_w, ada_b.reshape(DEPTH, 1, 6 * D_MODEL))


def _mlp_rows(x_ref, sh_ref, sc_ref, g_ref, o_ref, n2_ref, n3_ref, wup_ref, wdn_ref, batch, time_major):
    n_sub = _subtiles(x_ref.size // D_MODEL)
    for s in range(n_sub):
        x = _load_rows(x_ref, s, n_sub, time_major)
        hf = _norm_modulate(x, n2_ref[...], sc_ref[...], sh_ref[...], not time_major)
        hid = jnp.square(jnp.maximum(_mm(hf, wup_ref[...]), 0.0))
        f = _mm(hid, wdn_ref[...])
        y = _norm_gated_residual(x, g_ref[...], f, n3_ref[...], not time_major)
        _store_rows(o_ref, s, n_sub, y, batch, time_major)


def _mlp_kernel(xp_ref, shp_ref, scp_ref, gp_ref, xs_ref, shs_ref, scs_ref, gs_ref, n2_ref, n3_ref,
                wup_in, wdn_in, op_ref, os_ref, wup_s, wdn_s, *, n_tiles, batch_p, time_major_p, batch_s):
    _phases(
        n_tiles,
        lambda step: _cast_weight_chunks(step, (wup_in, wdn_in), (wup_s, wdn_s)),
        lambda: _mlp_rows(xp_ref, shp_ref, scp_ref, gp_ref, op_ref, n2_ref, n3_ref, wup_s, wdn_s,
                          batch_p, time_major_p),
        lambda: _mlp_rows(xs_ref, shs_ref, scs_ref, gs_ref, os_ref, n2_ref, n3_ref, wup_s, wdn_s,
                          batch_s, True))


def _mlp_call(xp, mod_p, xs, mod_s, norm_g, weights, layer, out_shape, batch_p, time_major_p, tiles_per_seq, name):
    n_tiles = xp.size // (MLP_TILE_ROWS * D_MODEL)
    w_specs, w_scratch = _streamed_weights(weights, layer)
    return pl.pallas_call(
        functools.partial(_mlp_kernel, n_tiles=n_tiles, batch_p=batch_p, time_major_p=time_major_p,
                          batch_s=mod_s.shape[1]),
        grid=(CAST_STEPS + n_tiles + 1,),
        in_specs=[_act_spec(xp.shape, MLP_TILE_ROWS), *_mod_specs(mod_p, layer, 3, n_tiles, tiles_per_seq),
                  _const_spec(xs.shape), *_mod_specs(mod_s, layer, 3, n_tiles),
                  _norm_spec(layer, 2), _norm_spec(layer, 3), *w_specs],
        out_specs=[_act_spec(out_shape, MLP_TILE_ROWS), pl.BlockSpec(xs.shape, lambda i: (0, 0))],
        out_shape=[jax.ShapeDtypeStruct(out_shape, F32), jax.ShapeDtypeStruct(xs.shape, F32)],
        scratch_shapes=w_scratch,
        compiler_params=_PARAMS,
        name=name,
    )(xp, mod_p, mod_p, mod_p, xs, mod_s, mod_s, mod_s, norm_g, norm_g, *weights)


def _lru_rows(x_ref, sh_ref, sc_ref, g_ref, o_ref, tail, h, p, hs_ref, steps, batch):
    n0_ref, n1_ref, cw_ref, cb_ref, wa_ref, ba_ref, wx_ref, bx_ref, lam_ref, win_ref, wout_ref = p
    nlam = -lam_ref[...]
    neg_c_softplus = -LRU_C * (jnp.maximum(nlam, 0.0) + jnp.log1p(jnp.exp(-jnp.abs(nlam))))
    cw = cw_ref[...]
    n_sub = _subtiles(steps * batch)
    sub_steps = steps // n_sub
    rows = sub_steps * batch
    for s in range(n_sub):
        r0 = s * rows
        x = _load_rows(x_ref, s, n_sub, True)
        hin = _norm_modulate(x, n0_ref[...], sc_ref[...], sh_ref[...])
        z = _mm(hin, win_ref[...])
        gate_br = z[:, :D_RNN]
        x_br = z[:, D_RNN:]

        xp = jnp.concatenate([tail, x_br], axis=0)
        xc = cb_ref[...]
        for k in range(CONV_W):
            xc = xc + xp[k * batch:k * batch + rows] * cw[k:k + 1]
        tail = xp[rows:]

        xcb = xc.astype(BF16)
        r = jax.nn.sigmoid(_block_diag_mm(xcb, wa_ref, LRU_BLOCK) + ba_ref[...])
        ig = jax.nn.sigmoid(_block_diag_mm(xcb, wx_ref, LRU_BLOCK) + bx_ref[...])
        log_a = r * neg_c_softplus
        a = jnp.exp(log_a)
        q = jnp.tanh(log_a) * (-1.0 - a * a)
        mult = jnp.where(q > 0.0, q * lax.rsqrt(q), 0.0)
        bt = mult * (ig * xc)

        for t in range(sub_steps):
            h = a[t * batch:(t + 1) * batch] * h + bt[t * batch:(t + 1) * batch]
            hs_ref[r0 + t * batch:r0 + (t + 1) * batch, :] = h

        y = hs_ref[r0:r0 + rows, :] * _gelu(gate_br)
        mix = _mm(y, wout_ref[...])
        o_ref[r0:r0 + rows, :] = _norm_gated_residual(x, g_ref[...], mix, n1_ref[...])
    return tail, h


def _lru_kernel(xp_ref, shp_ref, scp_ref, gp_ref, xs_ref, shs_ref, scs_ref, gs_ref, n0_ref, n1_ref,
                cw_ref, cb_ref, wa_ref, ba_ref, wx_ref, bx_ref, lam_ref, conv0p_ref, h0p_ref, conv0s_ref, h0s_ref,
                win_in, wout_in, op_ref, convp_ref, hp_ref, os_ref, convs_ref, hso_ref,
                tail_ref, hc_ref, hs_ref, win_s, wout_s, *, n_tiles, steps_p, batch_p, steps_s, batch_s):
    p = (n0_ref, n1_ref, cw_ref, cb_ref, wa_ref, ba_ref, wx_ref, bx_ref, lam_ref, win_s, wout_s)

    @pl.when(pl.program_id(0) == 0)
    def _():
        tail_ref[...] = conv0p_ref[...]
        hc_ref[...] = h0p_ref[...]

    def prompt():
        tail, h = _lru_rows(xp_ref, shp_ref, scp_ref, gp_ref, op_ref, tail_ref[...], hc_ref[...], p, hs_ref,
                            steps_p, batch_p)
        tail_ref[...] = tail
        convp_ref[...] = tail
        hc_ref[...] = h
        hp_ref[...] = h

    def sample():
        tail, h = _lru_rows(xs_ref, shs_ref, scs_ref, gs_ref, os_ref, conv0s_ref[...], h0s_ref[...], p, hs_ref,
                            steps_s, batch_s)
        convs_ref[...] = tail
        hso_ref[...] = h

    _phases(n_tiles, lambda step: _cast_weight_chunks(step, (win_in, wout_in), (win_s, wout_s)), prompt, sample)


def _lru_call(xp, mod_p, xs, mod_s, norm_g, p, weights, conv0_p, h0_p, conv0_s, h0_s, layer, name):
    j = layer // 2
    batch_p, batch_s = h0_p.shape[0], h0_s.shape[0]
    rows_s = xs.shape[0]
    n_tiles = xp.size // (LRU_TILE_ROWS * D_MODEL)
    w_specs, w_scratch = _streamed_weights(weights, j)
    row_spec = _act_spec((xp.size // D_MODEL, D_MODEL), LRU_TILE_ROWS)
    whole = lambda a: pl.BlockSpec(a.shape, lambda i: (0, 0))
    buf_rows = max(LRU_TILE_ROWS, rows_s)
    return pl.pallas_call(
        functools.partial(_lru_kernel, n_tiles=n_tiles, steps_p=LRU_TILE_ROWS // batch_p, batch_p=batch_p,
                          steps_s=rows_s // batch_s, batch_s=batch_s),
        grid=(CAST_STEPS + n_tiles + 1,),
        in_specs=[_act_spec(xp.shape, LRU_TILE_ROWS), *_mod_specs(mod_p, layer, 0, n_tiles),
                  _const_spec(xs.shape), *_mod_specs(mod_s, layer, 0, n_tiles),
                  _norm_spec(layer, 0), _norm_spec(layer, 1),
                  _layer_spec(p["conv_w"].shape, j), _layer_spec(p["conv_b"].shape, j),
                  _layer_spec(p["wa"].shape, j), _layer_spec(p["ba"].shape, j),
                  _layer_spec(p["wx"].shape, j), _layer_spec(p["bx"].shape, j), _layer_spec(p["lam"].shape, j),
                  _const_spec(conv0_p.shape), _const_spec(h0_p.shape),
                  _const_spec(conv0_s.shape), _const_spec(h0_s.shape), *w_specs],
        out_specs=[row_spec, whole(conv0_p), whole(h0_p), whole(xs), whole(conv0_s), whole(h0_s)],
        out_shape=[jax.ShapeDtypeStruct((xp.size // D_MODEL, D_MODEL), F32),
                   jax.ShapeDtypeStruct(conv0_p.shape, F32), jax.ShapeDtypeStruct(h0_p.shape, F32),
                   jax.ShapeDtypeStruct(xs.shape, F32),
                   jax.ShapeDtypeStruct(conv0_s.shape, F32), jax.ShapeDtypeStruct(h0_s.shape, F32)],
        scratch_shapes=[pltpu.VMEM(conv0_p.shape, F32), pltpu.VMEM(h0_p.shape, F32),
                        pltpu.VMEM((buf_rows, D_RNN), F32), *w_scratch],
        compiler_params=_PARAMS,
        name=name,
    )(xp, mod_p, mod_p, mod_p, xs, mod_s, mod_s, mod_s, norm_g, norm_g, p["conv_w"], p["conv_b"],
      p["wa"], p["ba"], p["wx"], p["bx"], p["lam"], conv0_p, h0_p, conv0_s, h0_s, *weights)


def _sgu_front(x, sh_ref, sc_ref, n0_ref, win_ref, bin_ref, lng_ref, lnb_ref):
    hin = _norm_modulate(x, n0_ref[...], sc_ref[...], sh_ref[...])
    z = _gelu(_mm(hin, win_ref[...]) + bin_ref[...])
    u = z[:, :D_SG]
    v = z[:, D_SG:]
    vc = v - jnp.mean(v, axis=-1, keepdims=True)
    var = jnp.mean(vc * vc, axis=-1, keepdims=True)
    vn = vc * lax.rsqrt(var + EPS) * lng_ref[...] + lnb_ref[...]
    return u, vn


def _sgu_prompt_rows(x_ref, sh_ref, sc_ref, g_ref, o_ref, front, n1_ref, wsp_ref, bsx_ref, wout_ref):
    lane = lax.broadcasted_iota(jnp.int32, (CHUNK, MXU_N), 1)
    n_sub = _subtiles(x_ref.shape[0])
    rows = x_ref.shape[0] // n_sub
    for s in range(n_sub):
        x = x_ref[s * rows:(s + 1) * rows, :]
        u, vn = _sgu_front(x, sh_ref, sc_ref, *front)
        mixed_rows = []
        for c in range(rows // CHUNK):
            tiles = []
            for c0 in range(0, D_SG, MXU_N):
                v_tile = vn[c * CHUNK:(c + 1) * CHUNK, c0:c0 + MXU_N].astype(BF16)
                acc = None
                for g in range(c0 // SG_GROUP, (c0 + MXU_N - 1) // SG_GROUP + 1):
                    in_group = (lane >= g * SG_GROUP - c0) & (lane < (g + 1) * SG_GROUP - c0)
                    part = jnp.dot(wsp_ref[g], jnp.where(in_group, v_tile, jnp.zeros_like(v_tile)),
                                   preferred_element_type=F32)
                    acc = part if acc is None else acc + part
                tiles.append(acc)
            mixed_rows.append(jnp.concatenate(tiles, axis=1) + bsx_ref[...])
        mixed = jnp.concatenate(mixed_rows, axis=0)
        mix = _mm(u * mixed, wout_ref[...])
        o_ref[s * rows:(s + 1) * rows, :] = _norm_gated_residual(x, g_ref[...], mix, n1_ref[...])


def _sgu_sample_rows(x_ref, sh_ref, sc_ref, g_ref, o_ref, v_ref, front, n1_ref, coef_ref, bias_ref, wout_ref, batch):
    x = x_ref[...]
    steps = x.shape[0] // batch
    u, vn = _sgu_front(x, sh_ref, sc_ref, *front)
    v_ref[...] = vn
    slabs = []
    for t in range(steps):
        acc = None
        for s in range(t + 1):
            term = coef_ref[t * steps + s:t * steps + s + 1, :] * vn[s * batch:(s + 1) * batch]
            acc = term if acc is None else acc + term
        slabs.append(acc + bias_ref[t:t + 1, :])
    mixed = jnp.concatenate(slabs, axis=0)
    mix = _mm(u * mixed, wout_ref[...])
    o_ref[...] = _norm_gated_residual(x, g_ref[...], mix, n1_ref[...])


def _sgu_kernel(xp_ref, shp_ref, scp_ref, gp_ref, xs_ref, shs_ref, scs_ref, gs_ref, n0_ref, n1_ref,
                bin_ref, lng_ref, lnb_ref, ws_ref, bsx_ref, coef_ref, bias_ref, win_in, wout_in,
                op_ref, os_ref, vs_ref, wsp_ref, win_s, wout_s, *, n_tiles, batch_s):
    front = (n0_ref, win_s, bin_ref, lng_ref, lnb_ref)

    @pl.when(pl.program_id(0) == 0)
    def _():
        t_idx = lax.broadcasted_iota(jnp.int32, (CHUNK, CHUNK), 0)
        s_idx = lax.broadcasted_iota(jnp.int32, (CHUNK, CHUNK), 1)
        for g in range(D_SG // SG_GROUP):
            wsp_ref[g] = jnp.where(s_idx <= t_idx, ws_ref[g], 0.0).astype(BF16)

    _phases(
        n_tiles,
        lambda step: _cast_weight_chunks(step, (win_in, wout_in), (win_s, wout_s)),
        lambda: _sgu_prompt_rows(xp_ref, shp_ref, scp_ref, gp_ref, op_ref, front, n1_ref, wsp_ref, bsx_ref, wout_s),
        lambda: _sgu_sample_rows(xs_ref, shs_ref, scs_ref, gs_ref, os_ref, vs_ref, front, n1_ref, coef_ref,
                                 bias_ref, wout_s, batch_s))


def _sgu_call(xp, mod_p, xs, mod_s, norm_g, p, weights, layer, seq_len, name):
    j = layer // 2
    n_tiles = xp.shape[0] // SGU_TILE_ROWS
    w_specs, w_scratch = _streamed_weights(weights, j)
    v_shape = (xs.shape[0], D_SG)
    return pl.pallas_call(
        functools.partial(_sgu_kernel, n_tiles=n_tiles, batch_s=mod_s.shape[1]),
        grid=(CAST_STEPS + n_tiles + 1,),
        in_specs=[_act_spec(xp.shape, SGU_TILE_ROWS), *_mod_specs(mod_p, layer, 0, n_tiles, seq_len // SGU_TILE_ROWS),
                  _const_spec(xs.shape), *_mod_specs(mod_s, layer, 0, n_tiles),
                  _norm_spec(layer, 0), _norm_spec(layer, 1),
                  _layer_spec(p["b_in"].shape, j), _layer_spec(p["ln_g"].shape, j), _layer_spec(p["ln_b"].shape, j),
                  _layer_spec(p["ws"].shape, j), _layer_spec(p["bsx"].shape, j),
                  _layer_spec(p["coef"].shape, j), _layer_spec(p["bias"].shape, j), *w_specs],
        out_specs=[_act_spec(xp.shape, SGU_TILE_ROWS), pl.BlockSpec(xs.shape, lambda i: (0, 0)),
                   pl.BlockSpec(v_shape, lambda i: (0, 0))],
        out_shape=[jax.ShapeDtypeStruct(xp.shape, F32), jax.ShapeDtypeStruct(xs.shape, F32),
                   jax.ShapeDtypeStruct(v_shape, F32)],
        scratch_shapes=[pltpu.VMEM((D_SG // SG_GROUP, CHUNK, CHUNK), BF16), *w_scratch],
        compiler_params=_PARAMS,
        name=name,
    )(xp, mod_p, mod_p, mod_p, xs, mod_s, mod_s, mod_s, norm_g, norm_g, p["b_in"], p["ln_g"], p["ln_b"],
      p["ws"], p["bsx"], p["coef"], p["bias"], *weights)


def _band_pack(w):
    n_l, n_h, b, _ = w.shape
    width = n_h * b
    rows = w.reshape(n_l, width, b)
    reps = -(-(MXU_N + b) // b)
    tiles = []
    for c0, k0 in _band_starts(width, b):
        band = jnp.concatenate([rows[:, k0:k0 + BAND_K, :]] * reps, axis=2)[:, :, c0 % b:c0 % b + MXU_N]
        r_blk = (k0 + lax.broadcasted_iota(jnp.int32, (BAND_K, MXU_N), 0)) // b
        c_blk = (c0 + lax.broadcasted_iota(jnp.int32, (BAND_K, MXU_N), 1)) // b
        tiles.append(jnp.where(r_blk == c_blk, band, 0.0))
    return jnp.stack(tiles, axis=1).astype(BF16)


def _to_time_major(x):
    n, t, d = x.shape
    return x.transpose(1, 0, 2).reshape(t * n, d)


def _from_time_major(x, n):
    rows, d = x.shape
    return x.reshape(rows // n, n, d).transpose(1, 0, 2)


def kernel(x_prompt, x_sample, c_prompt, c_sample, state_lru_h, state_lru_conv, ada_w, ada_b, norm_g, lru_w_in, lru_conv_w, lru_conv_b, lru_wa, lru_ba, lru_wx, lru_bx, lru_lambda, lru_w_out, sg_w_in, sg_b_in, sg_ln_g, sg_ln_b, sg_ws, sg_bs, sg_w_out, mlp_w_up, mlp_w_down):
    n_p, t_p, _ = x_prompt.shape
    n_s, t_s, _ = x_sample.shape
    n_lru = lru_w_in.shape[0]
    n_sg = sg_w_in.shape[0]

    mod_s, mod_p = _ada_call(jnp.concatenate([c_sample, c_prompt], axis=0), ada_w, ada_b, n_s, n_p)
    mod_p_seq = mod_p.reshape(DEPTH, n_p, 1, 6 * D_MODEL)
    norms = norm_g.reshape(DEPTH * 4, 1, D_MODEL)

    lru = dict(
        conv_w=lru_conv_w, conv_b=lru_conv_b.reshape(n_lru, 1, D_RNN),
        wa=_band_pack(lru_wa), ba=lru_ba.reshape(n_lru, 1, D_RNN),
        wx=_band_pack(lru_wx), bx=lru_bx.reshape(n_lru, 1, D_RNN),
        lam=lru_lambda.reshape(n_lru, 1, D_RNN))
    sgu = dict(
        b_in=sg_b_in.reshape(n_sg, 1, 2 * D_SG),
        ln_g=sg_ln_g.reshape(n_sg, 1, D_SG), ln_b=sg_ln_b.reshape(n_sg, 1, D_SG),
        ws=sg_ws,
        bsx=jnp.repeat(sg_bs.transpose(0, 2, 1), SG_GROUP, axis=2),
        coef=jnp.repeat(sg_ws[:, :, :t_s, :t_s].transpose(0, 2, 3, 1), SG_GROUP, axis=3).reshape(n_sg, t_s * t_s, D_SG),
        bias=jnp.repeat(sg_bs[:, :, :t_s].transpose(0, 2, 1), SG_GROUP, axis=2))

    natural = (n_p, t_p, D_MODEL)
    flat = (n_p * t_p, D_MODEL)
    mlp_w = (mlp_w_up, mlp_w_down)
    xp = x_prompt
    xs = _to_time_major(x_sample)
    zero_conv = jnp.zeros(((CONV_W - 1) * n_p, D_RNN), F32)
    zero_h = jnp.zeros((n_p, D_RNN), F32)
    h_p, conv_p, h_s, conv_s, v_s = [], [], [], [], []
    for layer in range(DEPTH):
        j = layer // 2
        if layer % 2 == 0:
            xp, cp, hp, xs, cs, hs = _lru_call(
                xp, mod_p, xs, mod_s, norms, lru, (lru_w_in, lru_w_out), zero_conv, zero_h,
                _to_time_major(state_lru_conv[j]), state_lru_h[j], layer, f"lru_{layer}")
            xp, xs = _mlp_call(xp, mod_p, xs, mod_s, norms, mlp_w, layer, natural, n_p, True, 1, f"mlp_{layer}")
            h_p.append(hp)
            conv_p.append(_from_time_major(cp, n_p))
            h_s.append(hs)
            conv_s.append(_from_time_major(cs, n_s))
        else:
            xp, xs, v = _sgu_call(xp.reshape(flat), mod_p_seq, xs, mod_s, norms, sgu, (sg_w_in, sg_w_out), layer,
                                  t_p, f"sgu_{layer}")
            v_s.append(_from_time_major(v, n_s))
            if layer + 1 < DEPTH:
                xp, xs = _mlp_call(xp.reshape(natural), mod_p, xs, mod_s, norms, mlp_w, layer, flat, n_p, False, 1,
                                   f"mlp_{layer}")
            else:
                xp, xs = _mlp_call(xp, mod_p_seq, xs, mod_s, norms, mlp_w, layer, flat, 1, True, t_p // MLP_TILE_ROWS,
                                   f"mlp_{layer}")

    y_prompt = xp.reshape(n_p, t_p, D_MODEL)
    y_sample = _from_time_major(xs, n_s)
    return (y_prompt, y_sample, jnp.stack(h_p), jnp.stack(conv_p), jnp.stack(h_s), jnp.stack(conv_s),
            jnp.stack(v_s))
```

```python
import functools

import jax
import jax.numpy as jnp
from jax import lax
from jax.experimental import pallas as pl
from jax.experimental.pallas import tpu as pltpu

D_MODEL = 1024
D_RNN = 1536
D_SG = 1536
D_FF = 4096
DEPTH = 4
CONV_W = 4
LRU_BLOCK = 96
SG_GROUP = 96
CHUNK = 128
LRU_C = 8.0
EPS = 1e-6

MXU_N = 256
LANES = 128
BAND_K = 512
VMEM_LIMIT_BYTES = 58 * 1024 * 1024
CAST_STEPS = 4
SUB_ROWS = 256
MLP_TILE_ROWS = 512
SGU_TILE_ROWS = 512
LRU_TILE_ROWS = 512

F32 = jnp.float32
BF16 = jnp.bfloat16


def _row_rsqrt_ms(x):
    return lax.rsqrt(jnp.mean(x * x, axis=-1, keepdims=True) + EPS)


def _slabs(fn, x, vecs, seq_major):
    b = vecs[0].shape[0]
    if b == 1:
        return fn(x, *vecs)
    rows, d = x.shape
    if seq_major:
        return fn(x.reshape(b, rows // b, d), *[v[:, None, :] for v in vecs]).reshape(rows, d)
    return fn(x.reshape(rows // b, b, d), *[v[None] for v in vecs]).reshape(rows, d)


def _norm_modulate(x, g, sc, sh, seq_major=False):
    return _slabs(lambda xs, scale, shift: xs * scale + shift, x * _row_rsqrt_ms(x), (g * (1.0 + sc), sh), seq_major)


def _norm_gated_residual(x, gate, y, g, seq_major=False):
    yn = _slabs(lambda ys, scale: ys * scale, y * _row_rsqrt_ms(y), (gate * g,), seq_major)
    return x + yn


def _load_rows(x_ref, s, n, time_major):
    if len(x_ref.shape) == 2:
        rows = x_ref.shape[0] // n
        return x_ref[s * rows:(s + 1) * rows, :]
    b, t, d = x_ref.shape
    ts = t // n
    x3 = x_ref[:, s * ts:(s + 1) * ts, :]
    if time_major:
        x3 = jnp.swapaxes(x3, 0, 1)
    return x3.reshape(b * ts, d)


def _store_rows(o_ref, s, n, y, batch, time_major):
    rows, d = y.shape
    if len(o_ref.shape) == 2:
        if batch > 1 and not time_major:
            y = jnp.swapaxes(y.reshape(batch, rows // batch, d), 0, 1).reshape(rows, d)
        o_ref[s * rows:(s + 1) * rows, :] = y
        return
    b, t, _ = o_ref.shape
    ts = t // n
    y3 = jnp.swapaxes(y.reshape(ts, b, d), 0, 1) if time_major else y.reshape(b, ts, d)
    o_ref[:, s * ts:(s + 1) * ts, :] = y3


def _gelu(x):
    k = 0.7978845608028654
    t = jnp.tanh(x * (k + (k * 0.044715) * (x * x)))
    return x * (0.5 + 0.5 * t)


def _mm(a, w):
    return jnp.dot(a.astype(BF16), w, preferred_element_type=F32)


def _band_starts(width, block):
    out = []
    for c0 in range(0, width, MXU_N):
        lo = (c0 // block) * block
        hi = ((c0 + MXU_N - 1) // block + 1) * block
        k0 = min((lo // LANES) * LANES, width - BAND_K)
        assert k0 <= lo and hi <= k0 + BAND_K
        out.append((c0, k0))
    return out


def _block_diag_mm(xb, w_ref, block):
    tiles = [jnp.dot(xb[:, k0:k0 + BAND_K], w_ref[t], preferred_element_type=F32)
             for t, (_, k0) in enumerate(_band_starts(xb.shape[1], block))]
    return jnp.concatenate(tiles, axis=1)


def _cast_weight_chunks(step, srcs, dsts):
    for src, dst in zip(srcs, dsts):
        rows = src.shape[0]
        dst[pl.ds(pl.multiple_of(step * rows, rows), rows), :] = src[...].astype(BF16)


def _phases(n_tiles, cast, prompt, sample):
    step = pl.program_id(0)
    pl.when(step < CAST_STEPS)(lambda: cast(step))
    pl.when((step >= CAST_STEPS) & (step < CAST_STEPS + n_tiles))(prompt)
    pl.when(step == CAST_STEPS + n_tiles)(sample)


def _const_spec(shape):
    zeros = (0,) * len(shape)
    return pl.BlockSpec(shape, lambda i: zeros, pipeline_mode=pl.Buffered(1))


def _layer_spec(shape, index):
    zeros = (0,) * (len(shape) - 1)
    return pl.BlockSpec((None, *shape[1:]), lambda i: (index, *zeros), pipeline_mode=pl.Buffered(1))


def _tile_index(i, n_tiles):
    return jnp.clip(i - CAST_STEPS, 0, n_tiles - 1)


def _subtiles(rows):
    return max(rows // SUB_ROWS, 1)


def _act_spec(shape, tile_rows):
    n_tiles = (shape[0] if len(shape) == 2 else shape[0] * shape[1]) // tile_rows
    if len(shape) == 2:
        return pl.BlockSpec((tile_rows, shape[1]), lambda i: (_tile_index(i, n_tiles), 0))
    n, _, d = shape
    return pl.BlockSpec((n, tile_rows // n, d), lambda i: (0, _tile_index(i, n_tiles), 0))


def _mod_specs(mod, layer, first, n_tiles, tiles_per_seq=1):
    if mod.ndim == 4:
        return [pl.BlockSpec((None, None, 1, D_MODEL),
                             lambda i, k=k: (layer, _tile_index(i, n_tiles) // tiles_per_seq, 0, k))
                for k in range(first, first + 3)]
    b = mod.shape[1]
    return [pl.BlockSpec((None, b, D_MODEL), lambda i, k=k: (layer, 0, k)) for k in range(first, first + 3)]


def _norm_spec(layer, which):
    return pl.BlockSpec((None, 1, D_MODEL), lambda i: (4 * layer + which, 0, 0))


def _streamed_weights(weights, index):
    in_specs, scratch = [], []
    for w in weights:
        _, r, n = w.shape
        in_specs.append(pl.BlockSpec((None, r // CAST_STEPS, n),
                                     lambda i: (index, jnp.minimum(i, CAST_STEPS - 1), 0)))
        scratch.append(pltpu.VMEM((r, n), BF16))
    return in_specs, scratch


_PARAMS = pltpu.CompilerParams(dimension_semantics=("arbitrary",), vmem_limit_bytes=VMEM_LIMIT_BYTES)


def _ada_kernel(c_ref, w_ref, b_ref, os_ref, op_ref):
    c = c_ref[...]
    s = c * jax.nn.sigmoid(c)
    m = _mm(s, w_ref[...].astype(BF16)) + b_ref[...]
    n_s = os_ref.shape[0]
    os_ref[...] = m[:n_s]
    op_ref[...] = m[n_s:]


def _ada_call(c_all, ada_w, ada_b, n_s, n_p):
    tn = 1536
    n_all = c_all.shape[0]
    return pl.pallas_call(
        _ada_kernel,
        grid=(DEPTH, 6 * D_MODEL // tn),
        in_specs=[
            pl.BlockSpec((n_all, D_MODEL), lambda l, n: (0, 0)),
            pl.BlockSpec((None, D_MODEL, tn), lambda l, n: (l, 0, n)),
            pl.BlockSpec((None, 1, tn), lambda l, n: (l, 0, n)),
        ],
        out_specs=[
            pl.BlockSpec((None, n_s, tn), lambda l, n: (l, 0, n)),
            pl.BlockSpec((None, n_p, tn), lambda l, n: (l, 0, n)),
        ],
        out_shape=[
            jax.ShapeDtypeStruct((DEPTH, n_s, 6 * D_MODEL), F32),
            jax.ShapeDtypeStruct((DEPTH, n_p, 6 * D_MODEL), F32),
        ],
        compiler_params=pltpu.CompilerParams(
            dimension_semantics=("arbitrary", "arbitrary"), vmem_limit_bytes=VMEM_LIMIT_BYTES),
        name="ada_mod",
    )(c_all, ada_w, ada_b.reshape(DEPTH, 1, 6 * D_MODEL))


def _mlp_rows(x_ref, sh_ref, sc_ref, g_ref, o_ref, n2_ref, n3_ref, wup_ref, wdn_ref, batch, time_major):
    n_sub = _subtiles(x_ref.size // D_MODEL)
    for s in range(n_sub):
        x = _load_rows(x_ref, s, n_sub, time_major)
        hf = _norm_modulate(x, n2_ref[...], sc_ref[...], sh_ref[...], not time_major)
        hid = jnp.square(jnp.maximum(_mm(hf, wup_ref[...]), 0.0))
        f = _mm(hid, wdn_ref[...])
        y = _norm_gated_residual(x, g_ref[...], f, n3_ref[...], not time_major)
        _store_rows(o_ref, s, n_sub, y, batch, time_major)


def _mlp_kernel(xp_ref, shp_ref, scp_ref, gp_ref, xs_ref, shs_ref, scs_ref, gs_ref, n2_ref, n3_ref,
                wup_in, wdn_in, op_ref, os_ref, wup_s, wdn_s, *, n_tiles, batch_p, time_major_p, batch_s):
    _phases(
        n_tiles,
        lambda step: _cast_weight_chunks(step, (wup_in, wdn_in), (wup_s, wdn_s)),
        lambda: _mlp_rows(xp_ref, shp_ref, scp_ref, gp_ref, op_ref, n2_ref, n3_ref, wup_s, wdn_s,
                          batch_p, time_major_p),
        lambda: _mlp_rows(xs_ref, shs_ref, scs_ref, gs_ref, os_ref, n2_ref, n3_ref, wup_s, wdn_s,
                          batch_s, True))


def _mlp_call(xp, mod_p, xs, mod_s, norm_g, weights, layer, out_shape, batch_p, time_major_p, tiles_per_seq, name):
    n_tiles = xp.size // (MLP_TILE_ROWS * D_MODEL)
    w_specs, w_scratch = _streamed_weights(weights, layer)
    return pl.pallas_call(
        functools.partial(_mlp_kernel, n_tiles=n_tiles, batch_p=batch_p, time_major_p=time_major_p,
                          batch_s=mod_s.shape[1]),
        grid=(CAST_STEPS + n_tiles + 1,),
        in_specs=[_act_spec(xp.shape, MLP_TILE_ROWS), *_mod_specs(mod_p, layer, 3, n_tiles, tiles_per_seq),
                  _const_spec(xs.shape), *_mod_specs(mod_s, layer, 3, n_tiles),
                  _norm_spec(layer, 2), _norm_spec(layer, 3), *w_specs],
        out_specs=[_act_spec(out_shape, MLP_TILE_ROWS), pl.BlockSpec(xs.shape, lambda i: (0, 0))],
        out_shape=[jax.ShapeDtypeStruct(out_shape, F32), jax.ShapeDtypeStruct(xs.shape, F32)],
        scratch_shapes=w_scratch,
        compiler_params=_PARAMS,
        name=name,
    )(xp, mod_p, mod_p, mod_p, xs, mod_s, mod_s, mod_s, norm_g, norm_g, *weights)


def _lru_rows(x_ref, sh_ref, sc_ref, g_ref, o_ref, tail, h, p, hs_ref, steps, batch):
    n0_ref, n1_ref, cw_ref, cb_ref, wa_ref, ba_ref, wx_ref, bx_ref, lam_ref, win_ref, wout_ref = p
    nlam = -lam_ref[...]
    neg_c_softplus = -LRU_C * (jnp.maximum(nlam, 0.0) + jnp.log1p(jnp.exp(-jnp.abs(nlam))))
    cw = cw_ref[...]
    n_sub = _subtiles(steps * batch)
    sub_steps = steps // n_sub
    rows = sub_steps * batch
    for s in range(n_sub):
        r0 = s * rows
        x = _load_rows(x_ref, s, n_sub, True)
        hin = _norm_modulate(x, n0_ref[...], sc_ref[...], sh_ref[...])
        z = _mm(hin, win_ref[...])
        gate_br = z[:, :D_RNN]
        x_br = z[:, D_RNN:]

        xp = jnp.concatenate([tail, x_br], axis=0)
        xc = cb_ref[...]
        for k in range(CONV_W):
            xc = xc + xp[k * batch:k * batch + rows] * cw[k:k + 1]
        tail = xp[rows:]

        xcb = xc.astype(BF16)
        r = jax.nn.sigmoid(_block_diag_mm(xcb, wa_ref, LRU_BLOCK) + ba_ref[...])
        ig = jax.nn.sigmoid(_block_diag_mm(xcb, wx_ref, LRU_BLOCK) + bx_ref[...])
        log_a = r * neg_c_softplus
        a = jnp.exp(log_a)
        q = jnp.tanh(log_a) * (-1.0 - a * a)
        mult = jnp.where(q > 0.0, q * lax.rsqrt(q), 0.0)
        bt = mult * (ig * xc)

        for t in range(sub_steps):
            h = a[t * batch:(t + 1) * batch] * h + bt[t * batch:(t + 1) * batch]
            hs_ref[r0 + t * batch:r0 + (t + 1) * batch, :] = h

        y = hs_ref[r0:r0 + rows, :] * _gelu(gate_br)
        mix = _mm(y, wout_ref[...])
        o_ref[r0:r0 + rows, :] = _norm_gated_residual(x, g_ref[...], mix, n1_ref[...])
    return tail, h


def _lru_kernel(xp_ref, shp_ref, scp_ref, gp_ref, xs_ref, shs_ref, scs_ref, gs_ref, n0_ref, n1_ref,
                cw_ref, cb_ref, wa_ref, ba_ref, wx_ref, bx_ref, lam_ref, conv0p_ref, h0p_ref, conv0s_ref, h0s_ref,
                win_in, wout_in, op_ref, convp_ref, hp_ref, os_ref, convs_ref, hso_ref,
                tail_ref, hc_ref, hs_ref, win_s, wout_s, *, n_tiles, steps_p, batch_p, steps_s, batch_s):
    p = (n0_ref, n1_ref, cw_ref, cb_ref, wa_ref, ba_ref, wx_ref, bx_ref, lam_ref, win_s, wout_s)

    @pl.when(pl.program_id(0) == 0)
    def _():
        tail_ref[...] = conv0p_ref[...]
        hc_ref[...] = h0p_ref[...]

    def prompt():
        tail, h = _lru_rows(xp_ref, shp_ref, scp_ref, gp_ref, op_ref, tail_ref[...], hc_ref[...], p, hs_ref,
                            steps_p, batch_p)
        tail_ref[...] = tail
        convp_ref[...] = tail
        hc_ref[...] = h
        hp_ref[...] = h

    def sample():
        tail, h = _lru_rows(xs_ref, shs_ref, scs_ref, gs_ref, os_ref, conv0s_ref[...], h0s_ref[...], p, hs_ref,
                            steps_s, batch_s)
        convs_ref[...] = tail
        hso_ref[...] = h

    _phases(n_tiles, lambda step: _cast_weight_chunks(step, (win_in, wout_in), (win_s, wout_s)), prompt, sample)


def _lru_call(xp, mod_p, xs, mod_s, norm_g, p, weights, conv0_p, h0_p, conv0_s, h0_s, layer, name):
    j = layer // 2
    batch_p, batch_s = h0_p.shape[0], h0_s.shape[0]
    rows_s = xs.shape[0]
    n_tiles = xp.size // (LRU_TILE_ROWS * D_MODEL)
    w_specs, w_scratch = _streamed_weights(weights, j)
    row_spec = _act_spec((xp.size // D_MODEL, D_MODEL), LRU_TILE_ROWS)
    whole = lambda a: pl.BlockSpec(a.shape, lambda i: (0, 0))
    buf_rows = max(LRU_TILE_ROWS, rows_s)
    return pl.pallas_call(
        functools.partial(_lru_kernel, n_tiles=n_tiles, steps_p=LRU_TILE_ROWS // batch_p, batch_p=batch_p,
                          steps_s=rows_s // batch_s, batch_s=batch_s),
        grid=(CAST_STEPS + n_tiles + 1,),
        in_specs=[_act_spec(xp.shape, LRU_TILE_ROWS), *_mod_specs(mod_p, layer, 0, n_tiles),
                  _const_spec(xs.shape), *_mod_specs(mod_s, layer, 0, n_tiles),
                  _norm_spec(layer, 0), _norm_spec(layer, 1),
                  _layer_spec(p["conv_w"].shape, j), _layer_spec(p["conv_b"].shape, j),
                  _layer_spec(p["wa"].shape, j), _layer_spec(p["ba"].shape, j),
                  _layer_spec(p["wx"].shape, j), _layer_spec(p["bx"].shape, j), _layer_spec(p["lam"].shape, j),
                  _const_spec(conv0_p.shape), _const_spec(h0_p.shape),
                  _const_spec(conv0_s.shape), _const_spec(h0_s.shape), *w_specs],
        out_specs=[row_spec, whole(conv0_p), whole(h0_p), whole(xs), whole(conv0_s), whole(h0_s)],
        out_shape=[jax.ShapeDtypeStruct((xp.size // D_MODEL, D_MODEL), F32),
                   jax.ShapeDtypeStruct(conv0_p.shape, F32), jax.ShapeDtypeStruct(h0_p.shape, F32),
                   jax.ShapeDtypeStruct(xs.shape, F32),
                   jax.ShapeDtypeStruct(conv0_s.shape, F32), jax.ShapeDtypeStruct(h0_s.shape, F32)],
        scratch_shapes=[pltpu.VMEM(conv0_p.shape, F32), pltpu.VMEM(h0_p.shape, F32),
                        pltpu.VMEM((buf_rows, D_RNN), F32), *w_scratch],
        compiler_params=_PARAMS,
        name=name,
    )(xp, mod_p, mod_p, mod_p, xs, mod_s, mod_s, mod_s, norm_g, norm_g, p["conv_w"], p["conv_b"],
      p["wa"], p["ba"], p["wx"], p["bx"], p["lam"], conv0_p, h0_p, conv0_s, h0_s, *weights)


def _sgu_front(x, sh_ref, sc_ref, n0_ref, win_ref, bin_ref, lng_ref, lnb_ref):
    hin = _norm_modulate(x, n0_ref[...], sc_ref[...], sh_ref[...])
    z = _gelu(_mm(hin, win_ref[...]) + bin_ref[...])
    u = z[:, :D_SG]
    v = z[:, D_SG:]
    vc = v - jnp.mean(v, axis=-1, keepdims=True)
    var = jnp.mean(vc * vc, axis=-1, keepdims=True)
    vn = vc * lax.rsqrt(var + EPS) * lng_ref[...] + lnb_ref[...]
    return u, vn


def _sgu_prompt_rows(x_ref, sh_ref, sc_ref, g_ref, o_ref, front, n1_ref, wsp_ref, bsx_ref, wout_ref):
    lane = lax.broadcasted_iota(jnp.int32, (CHUNK, MXU_N), 1)
    n_sub = _subtiles(x_ref.shape[0])
    rows = x_ref.shape[0] // n_sub
    for s in range(n_sub):
        x = x_ref[s * rows:(s + 1) * rows, :]
        u, vn = _sgu_front(x, sh_ref, sc_ref, *front)
        mixed_rows = []
        for c in range(rows // CHUNK):
            tiles = []
            for c0 in range(0, D_SG, MXU_N):
                v_tile = vn[c * CHUNK:(c + 1) * CHUNK, c0:c0 + MXU_N].astype(BF16)
                acc = None
                for g in range(c0 // SG_GROUP, (c0 + MXU_N - 1) // SG_GROUP + 1):
                    in_group = (lane >= g * SG_GROUP - c0) & (lane < (g + 1) * SG_GROUP - c0)
                    part = jnp.dot(wsp_ref[g], jnp.where(in_group, v_tile, jnp.zeros_like(v_tile)),
                                   preferred_element_type=F32)
                    acc = part if acc is None else acc + part
                tiles.append(acc)
            mixed_rows.append(jnp.concatenate(tiles, axis=1) + bsx_ref[...])
        mixed = jnp.concatenate(mixed_rows, axis=0)
        mix = _mm(u * mixed, wout_ref[...])
        o_ref[s * rows:(s + 1) * rows, :] = _norm_gated_residual(x, g_ref[...], mix, n1_ref[...])


def _sgu_sample_rows(x_ref, sh_ref, sc_ref, g_ref, o_ref, v_ref, front, n1_ref, coef_ref, bias_ref, wout_ref, batch):
    x = x_ref[...]
    steps = x.shape[0] // batch
    u, vn = _sgu_front(x, sh_ref, sc_ref, *front)
    v_ref[...] = vn
    slabs = []
    for t in range(steps):
        acc = None
        for s in range(t + 1):
            term = coef_ref[t * steps + s:t * steps + s + 1, :] * vn[s * batch:(s + 1) * batch]
            acc = term if acc is None else acc + term
        slabs.append(acc + bias_ref[t:t + 1, :])
    mixed = jnp.concatenate(slabs, axis=0)
    mix = _mm(u * mixed, wout_ref[...])
    o_ref[...] = _norm_gated_residual(x, g_ref[...], mix, n1_ref[...])


def _sgu_kernel(xp_ref, shp_ref, scp_ref, gp_ref, xs_ref, shs_ref, scs_ref, gs_ref, n0_ref, n1_ref,
                bin_ref, lng_ref, lnb_ref, ws_ref, bsx_ref, coef_ref, bias_ref, win_in, wout_in,
                op_ref, os_ref, vs_ref, wsp_ref, win_s, wout_s, *, n_tiles, batch_s):
    front = (n0_ref, win_s, bin_ref, lng_ref, lnb_ref)

    @pl.when(pl.program_id(0) == 0)
    def _():
        t_idx = lax.broadcasted_iota(jnp.int32, (CHUNK, CHUNK), 0)
        s_idx = lax.broadcasted_iota(jnp.int32, (CHUNK, CHUNK), 1)
        for g in range(D_SG // SG_GROUP):
            wsp_ref[g] = jnp.where(s_idx <= t_idx, ws_ref[g], 0.0).astype(BF16)

    _phases(
        n_tiles,
        lambda step: _cast_weight_chunks(step, (win_in, wout_in), (win_s, wout_s)),
        lambda: _sgu_prompt_rows(xp_ref, shp_ref, scp_ref, gp_ref, op_ref, front, n1_ref, wsp_ref, bsx_ref, wout_s),
        lambda: _sgu_sample_rows(xs_ref, shs_ref, scs_ref, gs_ref, os_ref, vs_ref, front, n1_ref, coef_ref,
                                 bias_ref, wout_s, batch_s))


def _sgu_call(xp, mod_p, xs, mod_s, norm_g, p, weights, layer, seq_len, name):
    j = layer // 2
    n_tiles = xp.shape[0] // SGU_TILE_ROWS
    w_specs, w_scratch = _streamed_weights(weights, j)
    v_shape = (xs.shape[0], D_SG)
    return pl.pallas_call(
        functools.partial(_sgu_kernel, n_tiles=n_tiles, batch_s=mod_s.shape[1]),
        grid=(CAST_STEPS + n_tiles + 1,),
        in_specs=[_act_spec(xp.shape, SGU_TILE_ROWS), *_mod_specs(mod_p, layer, 0, n_tiles, seq_len // SGU_TILE_ROWS),
                  _const_spec(xs.shape), *_mod_specs(mod_s, layer, 0, n_tiles),
                  _norm_spec(layer, 0), _norm_spec(layer, 1),
                  _layer_spec(p["b_in"].shape, j), _layer_spec(p["ln_g"].shape, j), _layer_spec(p["ln_b"].shape, j),
                  _layer_spec(p["ws"].shape, j), _layer_spec(p["bsx"].shape, j),
                  _layer_spec(p["coef"].shape, j), _layer_spec(p["bias"].shape, j), *w_specs],
        out_specs=[_act_spec(xp.shape, SGU_TILE_ROWS), pl.BlockSpec(xs.shape, lambda i: (0, 0)),
                   pl.BlockSpec(v_shape, lambda i: (0, 0))],
        out_shape=[jax.ShapeDtypeStruct(xp.shape, F32), jax.ShapeDtypeStruct(xs.shape, F32),
                   jax.ShapeDtypeStruct(v_shape, F32)],
        scratch_shapes=[pltpu.VMEM((D_SG // SG_GROUP, CHUNK, CHUNK), BF16), *w_scratch],
        compiler_params=_PARAMS,
        name=name,
    )(xp, mod_p, mod_p, mod_p, xs, mod_s, mod_s, mod_s, norm_g, norm_g, p["b_in"], p["ln_g"], p["ln_b"],
      p["ws"], p["bsx"], p["coef"], p["bias"], *weights)


def _band_pack(w):
    n_l, n_h, b, _ = w.shape
    width = n_h * b
    rows = w.reshape(n_l, width, b)
    reps = -(-(MXU_N + b) // b)
    tiles = []
    for c0, k0 in _band_starts(width, b):
        band = jnp.concatenate([rows[:, k0:k0 + BAND_K, :]] * reps, axis=2)[:, :, c0 % b:c0 % b + MXU_N]
        r_blk = (k0 + lax.broadcasted_iota(jnp.int32, (BAND_K, MXU_N), 0)) // b
        c_blk = (c0 + lax.broadcasted_iota(jnp.int32, (BAND_K, MXU_N), 1)) // b
        tiles.append(jnp.where(r_blk == c_blk, band, 0.0))
    return jnp.stack(tiles, axis=1).astype(BF16)


def _to_time_major(x):
    n, t, d = x.shape
    return x.transpose(1, 0, 2).reshape(t * n, d)


def _from_time_major(x, n):
    rows, d = x.shape
    return x.reshape(rows // n, n, d).transpose(1, 0, 2)


def kernel(x_prompt, x_sample, c_prompt, c_sample, state_lru_h, state_lru_conv, ada_w, ada_b, norm_g, lru_w_in, lru_conv_w, lru_conv_b, lru_wa, lru_ba, lru_wx, lru_bx, lru_lambda, lru_w_out, sg_w_in, sg_b_in, sg_ln_g, sg_ln_b, sg_ws, sg_bs, sg_w_out, mlp_w_up, mlp_w_down):
    n_p, t_p, _ = x_prompt.shape
    n_s, t_s, _ = x_sample.shape
    n_lru = lru_w_in.shape[0]
    n_sg = sg_w_in.shape[0]

    mod_s, mod_p = _ada_call(jnp.concatenate([c_sample, c_prompt], axis=0), ada_w, ada_b, n_s, n_p)
    mod_p_seq = mod_p.reshape(DEPTH, n_p, 1, 6 * D_MODEL)
    norms = norm_g.reshape(DEPTH * 4, 1, D_MODEL)

    lru = dict(
        conv_w=lru_conv_w, conv_b=lru_conv_b.reshape(n_lru, 1, D_RNN),
        wa=_band_pack(lru_wa), ba=lru_ba.reshape(n_lru, 1, D_RNN),
        wx=_band_pack(lru_wx), bx=lru_bx.reshape(n_lru, 1, D_RNN),
        lam=lru_lambda.reshape(n_lru, 1, D_RNN))
    sgu = dict(
        b_in=sg_b_in.reshape(n_sg, 1, 2 * D_SG),
        ln_g=sg_ln_g.reshape(n_sg, 1, D_SG), ln_b=sg_ln_b.reshape(n_sg, 1, D_SG),
        ws=sg_ws,
        bsx=jnp.repeat(sg_bs.transpose(0, 2, 1), SG_GROUP, axis=2),
        coef=jnp.repeat(sg_ws[:, :, :t_s, :t_s].transpose(0, 2, 3, 1), SG_GROUP, axis=3).reshape(n_sg, t_s * t_s, D_SG),
        bias=jnp.repeat(sg_bs[:, :, :t_s].transpose(0, 2, 1), SG_GROUP, axis=2))

    natural = (n_p, t_p, D_MODEL)
    flat = (n_p * t_p, D_MODEL)
    mlp_w = (mlp_w_up, mlp_w_down)
    xp = x_prompt
    xs = _to_time_major(x_sample)
    zero_conv = jnp.zeros(((CONV_W - 1) * n_p, D_RNN), F32)
    zero_h = jnp.zeros((n_p, D_RNN), F32)
    h_p, conv_p, h_s, conv_s, v_s = [], [], [], [], []
    for layer in range(DEPTH):
        j = layer // 2
        if layer % 2 == 0:
            xp, cp, hp, xs, cs, hs = _lru_call(
                xp, mod_p, xs, mod_s, norms, lru, (lru_w_in, lru_w_out), zero_conv, zero_h,
                _to_time_major(state_lru_conv[j]), state_lru_h[j], layer, f"lru_{layer}")
            xp, xs = _mlp_call(xp, mod_p, xs, mod_s, norms, mlp_w, layer, natural, n_p, True, 1, f"mlp_{layer}")
            h_p.append(hp)
            conv_p.append(_from_time_major(cp, n_p))
            h_s.append(hs)
            conv_s.append(_from_time_major(cs, n_s))
        else:
            xp, xs, v = _sgu_call(xp.reshape(flat), mod_p_seq, xs, mod_s, norms, sgu, (sg_w_in, sg_w_out), layer,
                                  t_p, f"sgu_{layer}")
            v_s.append(_from_time_major(v, n_s))
            if layer + 1 < DEPTH:
                xp, xs = _mlp_call(xp.reshape(natural), mod_p, xs, mod_s, norms, mlp_w, layer, flat, n_p, False, 1,
                                   f"mlp_{layer}")
            else:
                xp, xs = _mlp_call(xp, mod_p_seq, xs, mod_s, norms, mlp_w, layer, flat, 1, True, t_p // MLP_TILE_ROWS,
                                   f"mlp_{layer}")

    y_prompt = xp.reshape(n_p, t_p, D_MODEL)
    y_sample = _from_time_major(xs, n_s)
    return (y_prompt, y_sample, jnp.stack(h_p), jnp.stack(conv_p), jnp.stack(h_s), jnp.stack(conv_s),
            jnp.stack(v_s))
```
